```python
import math
import jax
import jax.numpy as jnp
from jax import lax
import numpy as np

D_MODEL = 1024
BATCH = 16
SEQ = 4096
DEPTH = 4
DEC_BATCH = 8
DEC_SEQ = 64
PAST_LEN = 2048

CHUNK = 64
N_MEM = 256
H_GDN = 4
DK_GDN = D_MODEL // (2 * H_GDN)
DV_GDN = D_MODEL // (2 * H_GDN)
CONV_W = 4
W_GDN_QK = H_GDN * DK_GDN
W_GDN_V = H_GDN * DV_GDN
CONV_CH = 2 * W_GDN_QK + W_GDN_V
H_DIFF = 4
D_DIFF = D_MODEL // (4 * H_DIFF)
W_DIFF = H_DIFF * 2 * D_DIFF
MIX_W = W_GDN_V + W_DIFF
H_MEM = 4
D_MEM = D_MODEL // H_MEM
D_FF = 4 * D_MODEL
Q_BLOCK = 128
EPS = 1e-6
SPLIT_AT = (CONV_CH, CONV_CH + W_GDN_V, CONV_CH + W_GDN_V + H_GDN, CONV_CH + W_GDN_V + 2 * H_GDN,
            CONV_CH + W_GDN_V + 2 * H_GDN + W_DIFF, CONV_CH + W_GDN_V + 2 * H_GDN + 2 * W_DIFF)
IN_COLS = CONV_CH + W_GDN_V + 2 * H_GDN + 3 * W_DIFF

kernel_name = 'hymba_gdn_diffattn_streaming_step'


def _rmsnorm(x, g):
    xf = x.astype(jnp.float32)
    y = xf * lax.rsqrt(jnp.mean(xf * xf, axis=-1, keepdims=True) + EPS)
    return (y * g.astype(jnp.float32)).astype(x.dtype)


def _l2norm(x):
    return x * lax.rsqrt(jnp.sum(x * x, axis=-1, keepdims=True) + EPS)


def _causal_conv(x, w, buf):
    L = x.shape[1]
    xp = jnp.concatenate([buf.astype(x.dtype), x], axis=1)
    y = w[0] * xp[:, 0:L]
    for i in range(1, CONV_W):
        y = y + w[i] * xp[:, i:i + L]
    return y, xp[:, L:]


def _gdn_chunked(q, k, v, g, beta, S0):
    B, L, H, DK = q.shape
    DV = v.shape[-1]
    C = min(CHUNK, L)
    n = L // C

    def blocks(t):
        t = t.reshape((B, n, C, H) + t.shape[3:])
        return jnp.moveaxis(jnp.moveaxis(t, 1, 0), 3, 2)

    qc, kc, vc, gc, bc = blocks(q), blocks(k), blocks(v), blocks(g), blocks(beta)
    G = jnp.cumsum(gc, axis=-1)
    tril = jnp.tril(jnp.ones((C, C), bool))
    strict = jnp.tril(jnp.ones((C, C), bool), -1)
    diff = G[..., :, None] - G[..., None, :]
    gam = jnp.where(tril, jnp.exp(jnp.where(tril, diff, 0.0)), 0.0)
    kk = jnp.einsum('nbhid,nbhjd->nbhij', kc, kc)
    a_sys = jnp.where(strict, bc[..., :, None] * kk * gam, 0.0) + jnp.eye(C, dtype=jnp.float32)
    u = lax.linalg.triangular_solve(a_sys, bc[..., None] * vc, left_side=True, lower=True, unit_diagonal=True)
    w = lax.linalg.triangular_solve(a_sys, (bc * jnp.exp(G))[..., None] * kc, left_side=True, lower=True,
                                    unit_diagonal=True)
    qk = jnp.einsum('nbhid,nbhjd->nbhij', qc, kc) * gam
    qg = qc * jnp.exp(G)[..., None]
    kd = kc * jnp.exp(G[..., -1:] - G)[..., None]
    gl = jnp.exp(G[..., -1])

    def step(S, xs):
        u_c, w_c, qk_c, qg_c, kd_c, gl_c = xs
        vn = u_c - jnp.einsum('bhcd,bhde->bhce', w_c, S)
        o = jnp.einsum('bhcd,bhde->bhce', qg_c, S) + jnp.einsum('bhij,bhje->bhie', qk_c, vn)
        S = S * gl_c[..., None, None] + jnp.einsum('bhcd,bhce->bhde', kd_c, vn)
        return S, o

    S, o = lax.scan(step, S0, (u, w, qk, qg, kd, gl))
    o = jnp.moveaxis(jnp.moveaxis(o, 2, 3), 0, 1).reshape(B, L, H, DV)
    return o, S


def _diff_attention(q, k, v, q_pos, k_pos, lam):
    B, Lq = q.shape[0], q.shape[1]
    n_blk = max(Lq // Q_BLOCK, 1)
    blk = Lq // n_blk
    qb = jnp.moveaxis(q.reshape((B, n_blk, blk) + q.shape[2:]), 1, 0)
    pb = q_pos.reshape(n_blk, blk)
    kf = k.astype(jnp.float32)
    vf = v.astype(jnp.float32)
    scale = D_DIFF ** -0.5

    def one(args):
        qi, pi = args
        s = jnp.einsum('bqhcd,bkhcd->bhcqk', qi.astype(jnp.float32), kf) * scale
        visible = k_pos[None, :] < ((pi // CHUNK + 1) * CHUNK)[:, None]
        s = jnp.where(visible, s, -jnp.inf)
        p = jax.nn.softmax(s, axis=-1)
        a = p[:, :, 0] - lam * p[:, :, 1]
        return jnp.einsum('bhqk,bkhe->bqhe', a, vf)

    o = lax.map(one, (qb, pb))
    return jnp.moveaxis(o, 0, 1).reshape(B, Lq, H_DIFF, 2 * D_DIFF)


def _mem_kv(mem, g, wk, wv):
    B, M, _ = mem.shape
    m = _rmsnorm(mem, g)
    return (m @ wk).reshape(B, M, H_MEM, D_MEM), (m @ wv).reshape(B, M, H_MEM, D_MEM)


def _mem_attn(x, mk, mv, wq, wo):
    B, L, _ = x.shape
    q = (x @ wq).reshape(B, L, H_MEM, D_MEM)
    s = jnp.einsum('blhd,bmhd->bhlm', q.astype(jnp.float32), mk.astype(jnp.float32)) * (D_MEM ** -0.5)
    p = jax.nn.softmax(s, axis=-1)
    o = jnp.einsum('bhlm,bmhd->blhd', p, mv.astype(jnp.float32)).astype(x.dtype)
    return o.reshape(B, L, D_MODEL) @ wo


def _layer(x, mem_k, mem_v, kc, vc, S0, conv_buf, lam_init, p):
    B, L, _ = x.shape
    dt = x.dtype
    xn = _rmsnorm(x, p['ln_mix'])
    proj = xn @ p['w_in']
    conv_in, z, a, b, qd, kd, vd = jnp.split(proj, SPLIT_AT, axis=-1)
    c, conv_new = _causal_conv(conv_in, p['conv_w'], conv_buf)
    c = jax.nn.silu(c.astype(jnp.float32))
    qa, ka, va = jnp.split(c, (W_GDN_QK, 2 * W_GDN_QK), axis=-1)
    qa = _l2norm(qa.reshape(B, L, H_GDN, DK_GDN)) * (DK_GDN ** -0.5)
    ka = _l2norm(ka.reshape(B, L, H_GDN, DK_GDN))
    va = va.reshape(B, L, H_GDN, DV_GDN)
    g = -jnp.exp(p['a_log'].astype(jnp.float32)) * jax.nn.softplus(a.astype(jnp.float32) + p['dt_bias'].astype(jnp.float32))
    beta = jax.nn.sigmoid(b.astype(jnp.float32))
    oa, S_new = _gdn_chunked(qa, ka, va, g, beta, S0.astype(jnp.float32))
    oa = _rmsnorm(oa, p['gdn_norm']) * jax.nn.silu(z.astype(jnp.float32).reshape(B, L, H_GDN, DV_GDN))
    oa = oa.reshape(B, L, W_GDN_V).astype(dt)
    kd = kd.reshape(B, L, H_DIFF, 2 * D_DIFF)
    vd = vd.reshape(B, L, H_DIFF, 2 * D_DIFF)
    if kc is None:
        k_all, v_all = kd, vd
    else:
        k_all = jnp.concatenate([kc.astype(dt), kd], axis=1)
        v_all = jnp.concatenate([vc.astype(dt), vd], axis=1)
    Lk = k_all.shape[1]
    k_pos = jnp.arange(Lk)
    q_pos = Lk - L + jnp.arange(L)
    lam = (jnp.exp(jnp.sum(p['lq1'].astype(jnp.float32) * p['lk1'].astype(jnp.float32)))
           - jnp.exp(jnp.sum(p['lq2'].astype(jnp.float32) * p['lk2'].astype(jnp.float32))) + lam_init)
    ob = _diff_attention(qd.reshape(B, L, H_DIFF, 2, D_DIFF), k_all.reshape(B, Lk, H_DIFF, 2, D_DIFF),
                         v_all, q_pos, k_pos, lam)
    ob = (_rmsnorm(ob, p['diff_norm']) * (1.0 - lam_init)).reshape(B, L, W_DIFF).astype(dt)
    h = x + jnp.concatenate([oa, ob], axis=-1) @ p['w_out']
    h = h + _mem_attn(_rmsnorm(h, p['ln_mem_q']), mem_k, mem_v, p['w_mem_q'], p['w_mem_o'])
    hn = _rmsnorm(h, p['ln_ffn'])
    h = h + jnp.square(jax.nn.relu(hn @ p['w_ff1'])) @ p['w_ff2']
    return h, kd, vd, S_new.astype(dt), conv_new


def setup_inputs(seed: int = 0) -> dict:
    key = jax.random.key(seed)
    ks = jax.random.split(key, 40)
    f32 = jnp.float32

    def nrm(k, shape, s):
        return jax.random.normal(k, shape, f32) * s

    def gain(k, shape):
        return 1.0 + 0.02 * jax.random.normal(k, shape, f32)

    dt0 = jnp.exp(jax.random.uniform(ks[9], (DEPTH, H_GDN), f32, math.log(1e-3), math.log(1e-1)))
    return {
        'x_prompt': nrm(ks[0], (BATCH, SEQ, D_MODEL), 1.0),
        'x_sample': nrm(ks[1], (DEC_BATCH, DEC_SEQ, D_MODEL), 1.0),
        'mem_prompt': nrm(ks[2], (BATCH, N_MEM, D_MODEL), 1.0),
        'cache_diff_k': nrm(ks[3], (DEPTH, DEC_BATCH, PAST_LEN, H_DIFF, 2 * D_DIFF), 1.0),
        'cache_diff_v': nrm(ks[4], (DEPTH, DEC_BATCH, PAST_LEN, H_DIFF, 2 * D_DIFF), 1.0),
        'cache_mem_k': nrm(ks[5], (DEPTH, DEC_BATCH, N_MEM, H_MEM, D_MEM), 1.0),
        'cache_mem_v': nrm(ks[6], (DEPTH, DEC_BATCH, N_MEM, H_MEM, D_MEM), 1.0),
        'state_gdn': nrm(ks[7], (DEPTH, DEC_BATCH, H_GDN, DK_GDN, DV_GDN), DK_GDN ** -0.5),
        'state_gdn_conv': nrm(ks[8], (DEPTH, DEC_BATCH, CONV_W - 1, CONV_CH), 1.0),
        'ln_mix': gain(ks[10], (DEPTH, D_MODEL)),
        'w_in': nrm(ks[11], (DEPTH, D_MODEL, IN_COLS), D_MODEL ** -0.5),
        'conv_w': nrm(ks[12], (DEPTH, CONV_W, CONV_CH), CONV_W ** -0.5),
        'a_log': jnp.log(jax.random.uniform(ks[13], (DEPTH, H_GDN), f32, 1.0, 16.0)),
        'dt_bias': dt0 + jnp.log(-jnp.expm1(-dt0)),
        'gdn_norm': gain(ks[14], (DEPTH, DV_GDN)),
        'lambda_q1': nrm(ks[15], (DEPTH, D_DIFF), 0.1),
        'lambda_k1': nrm(ks[16], (DEPTH, D_DIFF), 0.1),
        'lambda_q2': nrm(ks[17], (DEPTH, D_DIFF), 0.1),
        'lambda_k2': nrm(ks[18], (DEPTH, D_DIFF), 0.1),
        'diff_norm': gain(ks[19], (DEPTH, 2 * D_DIFF)),
        'w_out': nrm(ks[20], (DEPTH, MIX_W, D_MODEL), MIX_W ** -0.5),
        'ln_mem_q': gain(ks[21], (DEPTH, D_MODEL)),
        'ln_mem_kv': gain(ks[22], (DEPTH, D_MODEL)),
        'w_mem_q': nrm(ks[23], (DEPTH, D_MODEL, D_MODEL), D_MODEL ** -0.5),
        'w_mem_k': nrm(ks[24], (DEPTH, D_MODEL, D_MODEL), D_MODEL ** -0.5),
        'w_mem_v': nrm(ks[25], (DEPTH, D_MODEL, D_MODEL), D_MODEL ** -0.5),
        'w_mem_o': nrm(ks[26], (DEPTH, D_MODEL, D_MODEL), D_MODEL ** -0.5),
        'ln_ffn': gain(ks[27], (DEPTH, D_MODEL)),
        'w_ff1': nrm(ks[28], (DEPTH, D_MODEL, D_FF), D_MODEL ** -0.5),
        'w_ff2': nrm(ks[29], (DEPTH, D_FF, D_MODEL), D_FF ** -0.5),
        'ln_final': gain(ks[30], (D_MODEL,)),
    }


def reference(x_prompt, x_sample, mem_prompt, cache_diff_k, cache_diff_v, cache_mem_k, cache_mem_v,
              state_gdn, state_gdn_conv, ln_mix, w_in, conv_w, a_log, dt_bias, gdn_norm,
              lambda_q1, lambda_k1, lambda_q2, lambda_k2, diff_norm, w_out, ln_mem_q, ln_mem_kv,
              w_mem_q, w_mem_k, w_mem_v, w_mem_o, ln_ffn, w_ff1, w_ff2, ln_final):
    Bp = x_prompt.shape[0]
    hp, hs = x_prompt, x_sample
    pk, pv, pS, pc, pmk, pmv = [], [], [], [], [], []
    sk, sv, sS, sc = [], [], [], []
    for l in range(DEPTH):
        p = dict(ln_mix=ln_mix[l], w_in=w_in[l], conv_w=conv_w[l], a_log=a_log[l], dt_bias=dt_bias[l],
                 gdn_norm=gdn_norm[l], lq1=lambda_q1[l], lk1=lambda_k1[l], lq2=lambda_q2[l], lk2=lambda_k2[l],
                 diff_norm=diff_norm[l], w_out=w_out[l], ln_mem_q=ln_mem_q[l], w_mem_q=w_mem_q[l],
                 w_mem_o=w_mem_o[l], ln_ffn=ln_ffn[l], w_ff1=w_ff1[l], w_ff2=w_ff2[l])
        lam_init = 0.8 - 0.6 * math.exp(-0.3 * l)
        mk, mv = _mem_kv(mem_prompt, ln_mem_kv[l], w_mem_k[l], w_mem_v[l])
        S0 = jnp.zeros((Bp, H_GDN, DK_GDN, DV_GDN), jnp.float32)
        buf0 = jnp.zeros((Bp, CONV_W - 1, CONV_CH), x_prompt.dtype)
        hp, kp_, vp_, Sp_, cp_ = _layer(hp, mk, mv, None, None, S0, buf0, lam_init, p)
        pk.append(kp_); pv.append(vp_); pS.append(Sp_); pc.append(cp_); pmk.append(mk); pmv.append(mv)
        hs, ks_, vs_, Ss_, cs_ = _layer(hs, cache_mem_k[l], cache_mem_v[l], cache_diff_k[l], cache_diff_v[l],
                                        state_gdn[l], state_gdn_conv[l], lam_init, p)
        sk.append(ks_); sv.append(vs_); sS.append(Ss_); sc.append(cs_)
    y_prompt = _rmsnorm(hp, ln_final)
    y_sample = _rmsnorm(hs, ln_final)
    return (y_prompt, y_sample,
            jnp.stack(pk), jnp.stack(pv), jnp.stack(pS), jnp.stack(pc), jnp.stack(pmk), jnp.stack(pmv),
            jnp.stack(sk), jnp.stack(sv), jnp.stack(sS), jnp.stack(sc))
```

```python
import functools
import math

import jax
import jax.numpy as jnp
from jax import lax
from jax.experimental import pallas as pl
from jax.experimental.pallas import tpu as pltpu

D_MODEL = 1024
CHUNK = 64
H_GDN = 4
DK_GDN = 128
DV_GDN = 128
CONV_W = 4
W_GDN = H_GDN * DK_GDN
CONV_CH = 3 * W_GDN
H_DIFF = 4
D_DIFF = 64
HD_DIFF = 2 * D_DIFF
W_DIFF = H_DIFF * HD_DIFF
H_MEM = 4
D_MEM = 256
D_FF = 4 * D_MODEL
EPS = 1e-6
LOG2E = 1.4426950408889634
NEG = -1e30
MAIN_COLS = CONV_CH + 4 * 512
AB_OFF = CONV_CH + W_GDN
ROWS_GDN = H_GDN * CHUNK
V_ROWS = HD_DIFF + 16
ATT_TILE = 256
VMEM_LIMIT = 56 * 1024 * 1024

f32 = jnp.float32
bf16 = jnp.bfloat16


def _cparams(sem):
    return pltpu.CompilerParams(dimension_semantics=sem, vmem_limit_bytes=VMEM_LIMIT)


def _rms(x, g):
    ms = jnp.mean(x * x, axis=-1, keepdims=True)
    return x * lax.rsqrt(ms + EPS) * g


def _mm(a, b):
    return jnp.dot(a, b, preferred_element_type=f32)


def _mm_nt(a, b):
    return lax.dot_general(a, b, (((1,), (1,)), ((), ())), preferred_element_type=f32)


def _memkv_kernel(mem_ref, g_ref, wk_ref, wv_ref, mk_ref, mv_ref, mkb_ref, mvb_ref):
    xn = _rms(mem_ref[...], g_ref[0]).astype(bf16)
    k = _mm(xn, wk_ref[0])
    v = _mm(xn, wv_ref[0])
    mk_ref[0] = k
    mv_ref[0] = v
    mkb_ref[0] = k.astype(bf16)
    mvb_ref[0] = v.astype(bf16)


def _memkv(mem2d, ln, wk, wv):
    depth = wk.shape[0]
    t = mem2d.shape[0]
    tm = min(t, 512)
    out = jax.ShapeDtypeStruct((depth, t, D_MODEL), f32)
    outb = jax.ShapeDtypeStruct((depth, t, D_MODEL), bf16)
    wspec = pl.BlockSpec((1, D_MODEL, D_MODEL), lambda l, i: (l, 0, 0))
    ospec = pl.BlockSpec((1, tm, D_MODEL), lambda l, i: (l, i, 0))
    return pl.pallas_call(
        _memkv_kernel,
        grid=(depth, t // tm),
        in_specs=[pl.BlockSpec((tm, D_MODEL), lambda l, i: (i, 0)),
                  pl.BlockSpec((1, 1, D_MODEL), lambda l, i: (l, 0, 0)),
                  wspec, wspec],
        out_specs=[ospec, ospec, ospec, ospec],
        out_shape=[out, out, outb, outb],
        compiler_params=_cparams(("arbitrary", "arbitrary")),
        name="memkv",
    )(mem2d, ln, wk, wv)


def _inproj_kernel(x_ref, ln_ref, w_ref, wab_ref, cw_ref, alog_ref, dtb_ref, cbuf_ref,
                   cqkv_ref, z_ref, ab_ref, qs_ref, kd_ref, vd_ref, kb_ref, vb_ref, cnew_ref,
                   cs_ref, *, tm):
    i = pl.program_id(1)

    @pl.when(i == 0)
    def _():
        cs_ref[0:8, :] = cbuf_ref[0]

    xn = _rms(x_ref[0], ln_ref[...]).astype(bf16)
    conv_in = _mm(xn, w_ref[:, 0:CONV_CH])
    cs_ref[8:8 + tm, :] = conv_in
    y = cw_ref[0:1, :] * cs_ref[5:5 + tm, :]
    y = y + cw_ref[1:2, :] * cs_ref[6:6 + tm, :]
    y = y + cw_ref[2:3, :] * cs_ref[7:7 + tm, :]
    y = y + cw_ref[3:4, :] * conv_in
    tail = cs_ref[tm:tm + 8, :]
    cnew_ref[0] = tail
    cs_ref[0:8, :] = tail
    c = y * jax.nn.sigmoid(y)
    for h in range(H_GDN):
        for base, scale in ((0, DK_GDN ** -0.5), (W_GDN, 1.0)):
            lo = base + h * DK_GDN
            t = c[:, lo:lo + DK_GDN]
            n = t * lax.rsqrt(jnp.sum(t * t, axis=-1, keepdims=True) + EPS)
            cqkv_ref[0, :, lo:lo + DK_GDN] = n * scale if scale != 1.0 else n
    cqkv_ref[0, :, 2 * W_GDN:CONV_CH] = c[:, 2 * W_GDN:CONV_CH]

    z_ref[0] = _mm(xn, w_ref[:, CONV_CH:CONV_CH + 512])
    qd = _mm(xn, w_ref[:, CONV_CH + 512:CONV_CH + 1024])
    qs_ref[0] = (qd * (D_DIFF ** -0.5 * LOG2E)).astype(bf16)
    kd = _mm(xn, w_ref[:, CONV_CH + 1024:CONV_CH + 1536])
    kd_ref[0] = kd
    kb_ref[0] = kd.astype(bf16)
    vd = _mm(xn, w_ref[:, CONV_CH + 1536:CONV_CH + 2048])
    vd_ref[0] = vd
    vb_ref[0] = vd.astype(bf16)

    ab = _mm(xn, wab_ref[...])
    sp_in = ab + dtb_ref[...]
    softplus = jnp.maximum(sp_in, 0.0) + jnp.log1p(jnp.exp(-jnp.abs(sp_in)))
    g = -jnp.exp(alog_ref[...]) * softplus
    beta = jax.nn.sigmoid(ab)
    lane = lax.broadcasted_iota(jnp.int32, ab.shape, 1)
    ab_ref[0] = jnp.where(lane < H_GDN, g, beta)


def _inproj(x, ln, w_main, w_ab, conv_w, alog, dtb, cbuf8):
    b, l, _ = x.shape
    tm = min(l, 256)
    nt = l // tm
    tok = lambda w: pl.BlockSpec((1, tm, w), lambda bi, i: (bi, i, 0))
    full = lambda s: pl.BlockSpec(s, lambda bi, i: (0,) * len(s))
    out_shape = [
        jax.ShapeDtypeStruct((b, l, CONV_CH), f32),
        jax.ShapeDtypeStruct((b, l, W_GDN), f32),
        jax.ShapeDtypeStruct((b, l, 128), f32),
        jax.ShapeDtypeStruct((b, l, W_DIFF), bf16),
        jax.ShapeDtypeStruct((b, l, W_DIFF), f32),
        jax.ShapeDtypeStruct((b, l, W_DIFF), f32),
        jax.ShapeDtypeStruct((b, l, W_DIFF), bf16),
        jax.ShapeDtypeStruct((b, l, W_DIFF), bf16),
        jax.ShapeDtypeStruct((b, 8, CONV_CH), f32),
    ]
    out_specs = [tok(CONV_CH), tok(W_GDN), tok(128), tok(W_DIFF), tok(W_DIFF), tok(W_DIFF),
                 tok(W_DIFF), tok(W_DIFF),
                 pl.BlockSpec((1, 8, CONV_CH), lambda bi, i: (bi, 0, 0))]
    return pl.pallas_call(
        functools.partial(_inproj_kernel, tm=tm),
        grid=(b, nt),
        in_specs=[tok(D_MODEL), full((1, D_MODEL)), full((D_MODEL, MAIN_COLS)), full((D_MODEL, 128)),
                  full((CONV_W, CONV_CH)), full((1, 128)), full((1, 128)),
                  pl.BlockSpec((1, 8, CONV_CH), lambda bi, i: (bi, 0, 0))],
        out_specs=out_specs,
        out_shape=out_shape,
        scratch_shapes=[pltpu.VMEM((tm + 8, CONV_CH), f32)],
        compiler_params=_cparams(("arbitrary", "arbitrary")),
        name="inproj",
    )(x, ln, w_main, w_ab, conv_w, alog, dtb, cbuf8)


def _split3(x):
    hi = x.astype(bf16)
    r1 = x - hi.astype(f32)
    mid = r1.astype(bf16)
    lo = (r1 - mid.astype(f32)).astype(bf16)
    return hi, mid, lo


def _mmb(a, b):
    return _mm(a.astype(bf16), b.astype(bf16))


def _unit_lower_inverse(a, row, col):
    def blk(n):
        sh = n.bit_length() - 1
        return (row >> sh) == (col >> sh)

    eye = (row == col).astype(f32)
    a8 = jnp.where(blk(8), a, 0.0)
    p = _mmb(a8, a8)
    q = _mmb(p, p)
    t = _mmb(_mmb(eye - a8, eye + p), eye + q)
    for n in (8, 16, 32):
        off = jnp.where(jnp.logical_and(blk(2 * n), jnp.logical_not(blk(n))), a, 0.0)
        t = t - _mmb(_mmb(t, off), t)
    return t


def _stack_heads(x):
    return jnp.concatenate([x[:, h * 128:(h + 1) * 128] for h in range(H_GDN)], axis=0)


def _gdn_kernel(cqkv_ref, z_ref, ab_ref, s0_ref, gn_ref, oa_ref, s_ref, *, nc):
    i = pl.program_id(1)

    @pl.when(i == 0)
    def _():
        s_ref[...] = s0_ref[...]

    n = ROWS_GDN
    row = lax.broadcasted_iota(jnp.int32, (n, n), 0)
    col = lax.broadcasted_iota(jnp.int32, (n, n), 1)
    same = (row >> 6) == (col >> 6)
    lower = jnp.logical_and(same, row >= col)
    strict = jnp.logical_and(same, row > col)
    r64 = lax.broadcasted_iota(jnp.int32, (CHUNK, CHUNK), 0)
    c64 = lax.broadcasted_iota(jnp.int32, (CHUNK, CHUNK), 1)
    tril64 = (r64 >= c64).astype(bf16)
    head_of_row = lax.broadcasted_iota(jnp.int32, (n, DV_GDN), 0) >> 6

    def bcast_col(m, lane):
        return jnp.concatenate(
            [jnp.broadcast_to(m[:, lane + h:lane + h + 1], (m.shape[0], 128)) for h in range(H_GDN)], axis=0)

    for c in range(nc):
        rows = slice(c * CHUNK, (c + 1) * CHUNK)
        gb = ab_ref[0, rows, :]
        hi, mid, lo = _split3(gb)
        gcum = _mm(tril64, hi) + _mm(tril64, mid) + _mm(tril64, lo)
        gs = bcast_col(gcum, 0)
        bs = bcast_col(gb, H_GDN)
        gl = bcast_col(jnp.broadcast_to(gcum[CHUNK - 1:CHUNK, :], (CHUNK, 128)), 0)
        grow = gs.T[0:1, :]
        diff = jnp.concatenate([gs, gs], axis=1) - grow
        gam = jnp.where(lower, jnp.exp(jnp.where(lower, diff, 0.0)), 0.0)

        ks = _stack_heads(cqkv_ref[0, rows, W_GDN:2 * W_GDN])
        qs = _stack_heads(cqkv_ref[0, rows, 0:W_GDN])
        vs = _stack_heads(cqkv_ref[0, rows, 2 * W_GDN:CONV_CH])
        eg = jnp.exp(gs)
        kb = bs * eg * ks
        vb = bs * vs
        qg = qs * eg
        kdec = ks * jnp.exp(gl - gs)
        glast = jnp.exp(gl)

        kq = _mm_nt(jnp.concatenate([ks, qs], axis=0).astype(bf16), ks.astype(bf16))
        kk = kq[0:n]
        qk = kq[n:2 * n] * gam
        a = jnp.where(strict, jnp.concatenate([bs, bs], axis=1) * kk * gam, 0.0)
        t = _unit_lower_inverse(a, row, col)
        uw = _mmb(t, jnp.concatenate([vb, kb], axis=1))
        u = uw[:, 0:DV_GDN]
        w = uw[:, DV_GDN:]

        vn_parts, qs_parts = [], []
        for h in range(H_GDN):
            hr = slice(h * CHUNK, (h + 1) * CHUNK)
            r1 = _mmb(jnp.concatenate([w[hr], qg[hr]], axis=0), s_ref[0, h])
            vn_parts.append(u[hr] - r1[0:CHUNK])
            qs_parts.append(r1[CHUNK:])
        vn = jnp.concatenate(vn_parts, axis=0)
        o = jnp.concatenate(qs_parts, axis=0) + _mmb(qk, vn)
        kdec_t = kdec.T.astype(bf16)
        for h in range(H_GDN):
            hr = slice(h * CHUNK, (h + 1) * CHUNK)
            vn_h = jnp.where(head_of_row == h, vn, 0.0).astype(bf16)
            decay = jnp.concatenate([glast[hr], glast[hr]], axis=0)
            s_ref[0, h] = s_ref[0, h] * decay + _mm(kdec_t, vn_h)

        on = _rms(o, gn_ref[...])
        zs = _stack_heads(z_ref[0, rows, :])
        out = (on * (zs * jax.nn.sigmoid(zs))).astype(bf16)
        for h in range(H_GDN):
            oa_ref[0, rows, h * 128:(h + 1) * 128] = out[h * CHUNK:(h + 1) * CHUNK]


def _gdn(cqkv, z, ab, s0, gn):
    b, l, _ = cqkv.shape
    tg = min(l, 256)
    tok = lambda w: pl.BlockSpec((1, tg, w), lambda bi, i: (bi, i, 0))
    sspec = pl.BlockSpec((1, H_GDN, DK_GDN, DV_GDN), lambda bi, i: (bi, 0, 0, 0))
    return pl.pallas_call(
        functools.partial(_gdn_kernel, nc=tg // CHUNK),
        grid=(b, l // tg),
        in_specs=[tok(CONV_CH), tok(W_GDN), tok(128), sspec,
                  pl.BlockSpec((1, DV_GDN), lambda bi, i: (0, 0))],
        out_specs=[tok(W_GDN), sspec],
        out_shape=[jax.ShapeDtypeStruct((b, l, W_GDN), bf16),
                   jax.ShapeDtypeStruct((b, H_GDN, DK_GDN, DV_GDN), f32)],
        compiler_params=_cparams(("arbitrary", "arbitrary")),
        name="gdn",
    )(cqkv, z, ab, s0, gn)


def _attn_kernel(*refs, tq, tk, past, seq, lam_init):
    if past:
        lam_ref, gain_ref, q_ref, k_ref, v_ref, pk_ref, pv_ref, o_ref, ks_ref, vt_ref, acc_ref, m_ref = refs
    else:
        lam_ref, gain_ref, q_ref, k_ref, v_ref, o_ref, ks_ref, vt_ref, acc_ref, m_ref = refs
    i = pl.program_id(2)
    npast = past // tk
    tqe = max(tq, 128)

    @pl.when(i == 0)
    def _build():
        def put(t, kt, vtile):
            ks_ref[t] = kt.astype(bf16)
            vt_ref[t, 0:HD_DIFF, :] = vtile.astype(f32).T.astype(bf16)
            vt_ref[t, HD_DIFF:V_ROWS, :] = jnp.ones((V_ROWS - HD_DIFF, tk), bf16)

        for t in range(npast):
            put(t, pk_ref[0, t * tk:(t + 1) * tk, :], pv_ref[0, t * tk:(t + 1) * tk, :])
        if seq >= tk:
            for t in range(seq // tk):
                put(npast + t, k_ref[0, t * tk:(t + 1) * tk, :], v_ref[0, t * tk:(t + 1) * tk, :])
        else:
            zpad = jnp.zeros((tk - seq, HD_DIFF), bf16)
            put(npast, jnp.concatenate([k_ref[0], zpad], axis=0), jnp.concatenate([v_ref[0], zpad], axis=0))

    q = q_ref[0]
    if tqe > tq:
        q = jnp.concatenate([q, jnp.zeros((tqe - tq, HD_DIFF), bf16)], axis=0)
    lane = lax.broadcasted_iota(jnp.int32, q.shape, 1)
    zero = jnp.zeros_like(q)
    qp = jnp.concatenate([jnp.where(lane < D_DIFF, q, zero), jnp.where(lane >= D_DIFF, q, zero)], axis=0)

    acc_ref[...] = jnp.zeros_like(acc_ref)
    m_ref[...] = jnp.full(m_ref.shape, NEG, f32)

    krow = lax.broadcasted_iota(jnp.int32, (tk, 2 * tqe), 0)
    qcol = lax.broadcasted_iota(jnp.int32, (tk, 2 * tqe), 1)
    qcol = jnp.where(qcol >= tqe, qcol - tqe, qcol)
    visible = (krow >> 6) <= (qcol >> 6)

    def tile(j, masked):
        st = _mm_nt(ks_ref[j], qp)
        if masked:
            st = jnp.where(visible, st, NEG)
        m_old = m_ref[...]
        m_new = jnp.maximum(m_old, jnp.max(st, axis=0, keepdims=True))
        alpha = jnp.exp2(m_old - m_new)
        p = jnp.exp2(st - m_new).astype(bf16)
        acc_ref[...] = alpha * acc_ref[...] + _mm(vt_ref[j], p)
        m_ref[...] = m_new

    nfull = npast + (i if tq == tk else 0)

    def body(j, carry):
        tile(j, False)
        return carry

    lax.fori_loop(0, nfull, body, 0)
    tile(nfull, True)

    acc = acc_ref[...]
    lam = (jnp.exp(jnp.sum(lam_ref[0:1, :] * lam_ref[1:2, :], axis=-1, keepdims=True))
           - jnp.exp(jnp.sum(lam_ref[2:3, :] * lam_ref[3:4, :], axis=-1, keepdims=True)) + lam_init)
    o0 = acc[0:HD_DIFF, 0:tqe] / acc[HD_DIFF:HD_DIFF + 1, 0:tqe]
    o1 = acc[0:HD_DIFF, tqe:] / acc[HD_DIFF:HD_DIFF + 1, tqe:]
    ot = o0 - lam * o1
    ms = jnp.mean(ot * ot, axis=0, keepdims=True)
    o = (ot * lax.rsqrt(ms + EPS)).T * gain_ref[...] * (1.0 - lam_init)
    o_ref[0] = o[0:tq].astype(bf16)


def _attn(lamp, gain, qs, kb, vb, pk, pv, lam_init):
    b, l, _ = qs.shape
    tk = ATT_TILE
    tq = min(l, tk)
    past = 0 if pk is None else pk.shape[1]
    assert past % tk == 0 and (l % tk == 0 or l < tk)
    nt = past // tk + max(l // tk, 1)
    tqe = max(tq, 128)
    qspec = pl.BlockSpec((1, tq, HD_DIFF), lambda bi, h, i: (bi, i, h))
    kspec = pl.BlockSpec((1, l, HD_DIFF), lambda bi, h, i: (bi, 0, h))
    in_specs = [pl.BlockSpec((8, 128), lambda bi, h, i: (0, 0)),
                pl.BlockSpec((1, HD_DIFF), lambda bi, h, i: (0, 0)),
                qspec, kspec, kspec]
    args = [lamp, gain, qs, kb, vb]
    if past:
        pspec = pl.BlockSpec((1, past, HD_DIFF), lambda bi, h, i: (bi, 0, h))
        in_specs += [pspec, pspec]
        args += [pk, pv]
    return pl.pallas_call(
        functools.partial(_attn_kernel, tq=tq, tk=tk, past=past, seq=l, lam_init=lam_init),
        grid=(b, H_DIFF, l // tq),
        in_specs=in_specs,
        out_specs=qspec,
        out_shape=jax.ShapeDtypeStruct((b, l, W_DIFF), bf16),
        scratch_shapes=[pltpu.VMEM((nt, tk, HD_DIFF), bf16),
                        pltpu.VMEM((nt, V_ROWS, tk), bf16),
                        pltpu.VMEM((V_ROWS, 2 * tqe), f32),
                        pltpu.VMEM((1, 2 * tqe), f32)],
        compiler_params=_cparams(("arbitrary", "arbitrary", "arbitrary")),
        name="attn",
    )(*args)


def _mix_kernel(x_ref, oa_ref, ob_ref, wo_ref, ln_ref, wq_ref, mk_ref, mv_ref, wmo_ref, h_ref):
    h1 = x_ref[0] + _mm(oa_ref[0], wo_ref[0:W_GDN, :]) + _mm(ob_ref[0], wo_ref[W_GDN:, :])
    hn = _rms(h1, ln_ref[...]).astype(bf16)
    q = (_mm(hn, wq_ref[...]) * (D_MEM ** -0.5 * LOG2E)).astype(bf16)
    parts = []
    for h in range(H_MEM):
        cs = slice(h * D_MEM, (h + 1) * D_MEM)
        s = _mm_nt(q[:, cs], mk_ref[0, :, cs])
        p = jnp.exp2(s - jnp.max(s, axis=-1, keepdims=True))
        den = jnp.sum(p, axis=-1, keepdims=True)
        parts.append(_mm(p.astype(bf16), mv_ref[0, :, cs]) / den)
    o = jnp.concatenate(parts, axis=-1).astype(bf16)
    h_ref[0] = h1 + _mm(o, wmo_ref[...])


def _mix(x, oa, ob, w_out, ln, wq, mk, mv, wmo):
    b, l, _ = x.shape
    tm = min(l, 256)
    n_mem = mk.shape[1]
    tok = lambda w: pl.BlockSpec((1, tm, w), lambda bi, i: (bi, i, 0))
    full = lambda s: pl.BlockSpec(s, lambda bi, i: (0,) * len(s))
    mspec = pl.BlockSpec((1, n_mem, D_MODEL), lambda bi, i: (bi, 0, 0))
    return pl.pallas_call(
        _mix_kernel,
        grid=(b, l // tm),
        in_specs=[tok(D_MODEL), tok(W_GDN), tok(W_DIFF), full((D_MODEL, D_MODEL)), full((1, D_MODEL)),
                  full((D_MODEL, D_MODEL)), mspec, mspec, full((D_MODEL, D_MODEL))],
        out_specs=tok(D_MODEL),
        out_shape=jax.ShapeDtypeStruct((b, l, D_MODEL), f32),
        compiler_params=_cparams(("arbitrary", "arbitrary")),
        name="mix",
    )(x, oa, ob, w_out, ln, wq, mk, mv, wmo)


def _ffn_kernel(h_ref, ln_ref, w1_ref, w2_ref, lnf_ref, o_ref, *, final):
    h = h_ref[...]
    hn = _rms(h, ln_ref[...]).astype(bf16)
    acc = h
    step = 1024
    for c in range(D_FF // step):
        a = jnp.maximum(_mm(hn, w1_ref[:, c * step:(c + 1) * step]), 0.0)
        acc = acc + _mm((a * a).astype(bf16), w2_ref[c * step:(c + 1) * step, :])
    if final:
        acc = _rms(acc, lnf_ref[...])
    o_ref[...] = acc


def _ffn(h2d, ln, w1, w2, lnf, final):
    t = h2d.shape[0]
    tm = min(t, 256)
    full = lambda s: pl.BlockSpec(s, lambda i: (0,) * len(s))
    tok = pl.BlockSpec((tm, D_MODEL), lambda i: (i, 0))
    return pl.pallas_call(
        functools.partial(_ffn_kernel, final=final),
        grid=(t // tm,),
        in_specs=[tok, full((1, D_MODEL)), full((D_MODEL, D_FF)), full((D_FF, D_MODEL)), full((1, D_MODEL))],
        out_specs=tok,
        out_shape=jax.ShapeDtypeStruct((t, D_MODEL), f32),
        compiler_params=_cparams(("arbitrary",)),
        name="ffn",
    )(h2d, ln, w1, w2, lnf)


def _pad_lanes(x, width=128):
    return jnp.pad(x, ((0, 0), (0, width - x.shape[-1])))


def kernel(x_prompt, x_sample, mem_prompt, cache_diff_k, cache_diff_v, cache_mem_k, cache_mem_v,
           state_gdn, state_gdn_conv, ln_mix, w_in, conv_w, a_log, dt_bias, gdn_norm,
           lambda_q1, lambda_k1, lambda_q2, lambda_k2, diff_norm, w_out, ln_mem_q, ln_mem_kv,
           w_mem_q, w_mem_k, w_mem_v, w_mem_o, ln_ffn, w_ff1, w_ff2, ln_final):
    depth = w_in.shape[0]
    bp, lp, _ = x_prompt.shape
    bs, ls, _ = x_sample.shape
    n_mem = mem_prompt.shape[1]
    past = cache_diff_k.shape[2]

    w_main = jnp.concatenate([w_in[:, :, :AB_OFF], w_in[:, :, AB_OFF + 2 * H_GDN:]], axis=2).astype(bf16)
    w_ab = jnp.pad(w_in[:, :, AB_OFF:AB_OFF + 2 * H_GDN], ((0, 0), (0, 0), (0, 128 - 2 * H_GDN))).astype(bf16)
    w_out_b, wq_b, wk_b, wv_b, wmo_b = (w.astype(bf16) for w in (w_out, w_mem_q, w_mem_k, w_mem_v, w_mem_o))
    w1_b, w2_b = w_ff1.astype(bf16), w_ff2.astype(bf16)
    alog_p = _pad_lanes(a_log)
    dtb_p = _pad_lanes(dt_bias)
    lamp = jnp.stack([_pad_lanes(p) for p in (lambda_q1, lambda_k1, lambda_q2, lambda_k2)], axis=1)
    lamp = jnp.pad(lamp, ((0, 0), (0, 4), (0, 0)))
    lnf = ln_final.reshape(1, D_MODEL)

    mk_all, mv_all, mkb_all, mvb_all = _memkv(mem_prompt.reshape(bp * n_mem, D_MODEL),
                                              ln_mem_kv.reshape(depth, 1, D_MODEL), wk_b, wv_b)

    cbuf_p = jnp.zeros((bp, 8, CONV_CH), f32)
    s0_p = jnp.zeros((bp, H_GDN, DK_GDN, DV_GDN), f32)
    cbuf_s_all = jnp.pad(state_gdn_conv, ((0, 0), (0, 0), (8 - (CONV_W - 1), 0), (0, 0)))
    cmk_b = cache_mem_k.reshape(depth, bs, n_mem, D_MODEL).astype(bf16)
    cmv_b = cache_mem_v.reshape(depth, bs, n_mem, D_MODEL).astype(bf16)
    cdk = cache_diff_k.reshape(depth, bs, past, W_DIFF)
    cdv = cache_diff_v.reshape(depth, bs, past, W_DIFF)

    def layer(l, x, cbuf8, s0, pk, pv, mk, mv, final):
        b, ln_, _ = x.shape
        lam_init = 0.8 - 0.6 * math.exp(-0.3 * l)
        cqkv, z, ab, qs, kd, vd, kb, vb, cnew = _inproj(
            x, ln_mix[l].reshape(1, D_MODEL), w_main[l], w_ab[l], conv_w[l],
            alog_p[l:l + 1], dtb_p[l:l + 1], cbuf8)
        oa, s_new = _gdn(cqkv, z, ab, s0, gdn_norm[l].reshape(1, DV_GDN))
        ob = _attn(lamp[l], diff_norm[l].reshape(1, HD_DIFF), qs, kb, vb, pk, pv, lam_init)
        h = _mix(x, oa, ob, w_out_b[l], ln_mem_q[l].reshape(1, D_MODEL), wq_b[l], mk, mv, wmo_b[l])
        h = _ffn(h.reshape(b * ln_, D_MODEL), ln_ffn[l].reshape(1, D_MODEL), w1_b[l], w2_b[l], lnf, final)
        return (h.reshape(b, ln_, D_MODEL), kd.reshape(b, ln_, H_DIFF, HD_DIFF),
                vd.reshape(b, ln_, H_DIFF, HD_DIFF), s_new, cnew[:, 8 - (CONV_W - 1):, :])

    hp, hs = x_prompt, x_sample
    pk_, pv_, ps_, pc_, sk_, sv_, ss_, sc_ = [], [], [], [], [], [], [], []
    for l in range(depth):
        final = l == depth - 1
        hp, k_, v_, s_, c_ = layer(l, hp, cbuf_p, s0_p, None, None,
                                   mkb_all[l].reshape(bp, n_mem, D_MODEL),
                                   mvb_all[l].reshape(bp, n_mem, D_MODEL), final)
        pk_.append(k_); pv_.append(v_); ps_.append(s_); pc_.append(c_)
        hs, k_, v_, s_, c_ = layer(l, hs, cbuf_s_all[l], state_gdn[l], cdk[l], cdv[l],
                                   cmk_b[l], cmv_b[l], final)
        sk_.append(k_); sv_.append(v_); ss_.append(s_); sc_.append(c_)

    mem_shape = (depth, bp, n_mem, H_MEM, D_MEM)
    return (hp, hs,
            jnp.stack(pk_), jnp.stack(pv_), jnp.stack(ps_), jnp.stack(pc_),
            mk_all.reshape(mem_shape), mv_all.reshape(mem_shape),
            jnp.stack(sk_), jnp.stack(sv_), jnp.stack(ss_), jnp.stack(sc_))
```

```python
import functools
import math

import jax
import jax.numpy as jnp
from jax import lax
from jax.experimental import pallas as pl
from jax.experimental.pallas import tpu as pltpu

D_MODEL = 1024
CHUNK = 64
H_GDN = 4
DK_GDN = 128
DV_GDN = 128
CONV_W = 4
W_GDN = H_GDN * DK_GDN
CONV_CH = 3 * W_GDN
H_DIFF = 4
D_DIFF = 64
HD_DIFF = 2 * D_DIFF
W_DIFF = H_DIFF * HD_DIFF
H_MEM = 4
D_MEM = 256
D_FF = 4 * D_MODEL
EPS = 1e-6
LOG2E = 1.4426950408889634
NEG = -1e30
MAIN_COLS = CONV_CH + 4 * 512
AB_OFF = CONV_CH + W_GDN
ROWS_GDN = H_GDN * CHUNK
V_ROWS = HD_DIFF + 16
ATT_TILE = 256
VMEM_LIMIT = 56 * 1024 * 1024

f32 = jnp.float32
bf16 = jnp.bfloat16


def _cparams(sem):
    return pltpu.CompilerParams(dimension_semantics=sem, vmem_limit_bytes=VMEM_LIMIT)


def _rms(x, g):
    ms = jnp.mean(x * x, axis=-1, keepdims=True)
    return x * lax.rsqrt(ms + EPS) * g


def _mm(a, b):
    return jnp.dot(a, b, preferred_element_type=f32)


def _mm_nt(a, b):
    return lax.dot_general(a, b, (((1,), (1,)), ((), ())), preferred_element_type=f32)


def _memkv_kernel(mem_ref, g_ref, wk_ref, wv_ref, mk_ref, mv_ref, mkb_ref, mvb_ref):
    xn = _rms(mem_ref[...], g_ref[0]).astype(bf16)
    k = _mm(xn, wk_ref[0])
    v = _mm(xn, wv_ref[0])
    mk_ref[0] = k
    mv_ref[0] = v
    mkb_ref[0] = k.astype(bf16)
    mvb_ref[0] = v.astype(bf16)


def _memkv(mem2d, ln, wk, wv):
    depth = wk.shape[0]
    t = mem2d.shape[0]
    tm = min(t, 512)
    out = jax.ShapeDtypeStruct((depth, t, D_MODEL), f32)
    outb = jax.ShapeDtypeStruct((depth, t, D_MODEL), bf16)
    wspec = pl.BlockSpec((1, D_MODEL, D_MODEL), lambda l, i: (l, 0, 0))
    ospec = pl.BlockSpec((1, tm, D_MODEL), lambda l, i: (l, i, 0))
    return pl.pallas_call(
        _memkv_kernel,
        grid=(depth, t // tm),
        in_specs=[pl.BlockSpec((tm, D_MODEL), lambda l, i: (i, 0)),
                  pl.BlockSpec((1, 1, D_MODEL), lambda l, i: (l, 0, 0)),
                  wspec, wspec],
        out_specs=[ospec, ospec, ospec, ospec],
        out_shape=[out, out, outb, outb],
        compiler_params=_cparams(("arbitrary", "arbitrary")),
        name="memkv",
    )(mem2d, ln, wk, wv)


def _inproj_kernel(x_ref, ln_ref, w_ref, wab_ref, cw_ref, alog_ref, dtb_ref, cbuf_ref,
                   cqkv_ref, z_ref, ab_ref, qs_ref, kd_ref, vd_ref, kb_ref, vb_ref, cnew_ref,
                   cs_ref, *, tm):
    i = pl.program_id(1)

    @pl.when(i == 0)
    def _():
        cs_ref[0:8, :] = cbuf_ref[0]

    xn = _rms(x_ref[0], ln_ref[...]).astype(bf16)
    conv_in = _mm(xn, w_ref[:, 0:CONV_CH])
    cs_ref[8:8 + tm, :] = conv_in
    y = cw_ref[0:1, :] * cs_ref[5:5 + tm, :]
    y = y + cw_ref[1:2, :] * cs_ref[6:6 + tm, :]
    y = y + cw_ref[2:3, :] * cs_ref[7:7 + tm, :]
    y = y + cw_ref[3:4, :] * conv_in
    tail = cs_ref[tm:tm + 8, :]
    cnew_ref[0] = tail
    cs_ref[0:8, :] = tail
    c = y * jax.nn.sigmoid(y)
    for h in range(H_GDN):
        for base, scale in ((0, DK_GDN ** -0.5), (W_GDN, 1.0)):
            lo = base + h * DK_GDN
            t = c[:, lo:lo + DK_GDN]
            n = t * lax.rsqrt(jnp.sum(t * t, axis=-1, keepdims=True) + EPS)
            cqkv_ref[0, :, lo:lo + DK_GDN] = n * scale if scale != 1.0 else n
    cqkv_ref[0, :, 2 * W_GDN:CONV_CH] = c[:, 2 * W_GDN:CONV_CH]

    z_ref[0] = _mm(xn, w_ref[:, CONV_CH:CONV_CH + 512])
    qd = _mm(xn, w_ref[:, CONV_CH + 512:CONV_CH + 1024])
    qs_ref[0] = (qd * (D_DIFF ** -0.5 * LOG2E)).astype(bf16)
    kd = _mm(xn, w_ref[:, CONV_CH + 1024:CONV_CH + 1536])
    kd_ref[0] = kd
    kb_ref[0] = kd.astype(bf16)
    vd = _mm(xn, w_ref[:, CONV_CH + 1536:CONV_CH + 2048])
    vd_ref[0] = vd
    vb_ref[0] = vd.astype(bf16)

    ab = _mm(xn, wab_ref[...])
    sp_in = ab + dtb_ref[...]
    softplus = jnp.maximum(sp_in, 0.0) + jnp.log1p(jnp.exp(-jnp.abs(sp_in)))
    g = -jnp.exp(alog_ref[...]) * softplus
    beta = jax.nn.sigmoid(ab)
    lane = lax.broadcasted_iota(jnp.int32, ab.shape, 1)
    ab_ref[0] = jnp.where(lane < H_GDN, g, beta)


def _inproj(x, ln, w_main, w_ab, conv_w, alog, dtb, cbuf8):
    b, l, _ = x.shape
    tm = min(l, 256)
    nt = l // tm
    tok = lambda w: pl.BlockSpec((1, tm, w), lambda bi, i: (bi, i, 0))
    full = lambda s: pl.BlockSpec(s, lambda bi, i: (0,) * len(s))
    out_shape = [
        jax.ShapeDtypeStruct((b, l, CONV_CH), f32),
        jax.ShapeDtypeStruct((b, l, W_GDN), f32),
        jax.ShapeDtypeStruct((b, l, 128), f32),
        jax.ShapeDtypeStruct((b, l, W_DIFF), bf16),
        jax.ShapeDtypeStruct((b, l, W_DIFF), f32),
        jax.ShapeDtypeStruct((b, l, W_DIFF), f32),
        jax.ShapeDtypeStruct((b, l, W_DIFF), bf16),
        jax.ShapeDtypeStruct((b, l, W_DIFF), bf16),
        jax.ShapeDtypeStruct((b, 8, CONV_CH), f32),
    ]
    out_specs = [tok(CONV_CH), tok(W_GDN), tok(128), tok(W_DIFF), tok(W_DIFF), tok(W_DIFF),
                 tok(W_DIFF), tok(W_DIFF),
                 pl.BlockSpec((1, 8, CONV_CH), lambda bi, i: (bi, 0, 0))]
    return pl.pallas_call(
        functools.partial(_inproj_kernel, tm=tm),
        grid=(b, nt),
        in_specs=[tok(D_MODEL), full((1, D_MODEL)), full((D_MODEL, MAIN_COLS)), full((D_MODEL, 128)),
                  full((CONV_W, CONV_CH)), full((1, 128)), full((1, 128)),
                  pl.BlockSpec((1, 8, CONV_CH), lambda bi, i: (bi, 0, 0))],
        out_specs=out_specs,
        out_shape=out_shape,
        scratch_shapes=[pltpu.VMEM((tm + 8, CONV_CH), f32)],
        compiler_params=_cparams(("arbitrary", "arbitrary")),
        name="inproj",
    )(x, ln, w_main, w_ab, conv_w, alog, dtb, cbuf8)


def _split3(x):
    hi = x.astype(bf16)
    r1 = x - hi.astype(f32)
    mid = r1.astype(bf16)
    lo = (r1 - mid.astype(f32)).astype(bf16)
    return hi, mid, lo


def _mmb(a, b):
    return _mm(a.astype(bf16), b.astype(bf16))


def _unit_lower_inverses(mats, row, col):
    def blk(n):
        sh = n.bit_length() - 1
        return (row >> sh) == (col >> sh)

    eye = (row == col).astype(f32)
    a8 = [jnp.where(blk(8), a, 0.0) for a in mats]
    p = [_mmb(x, x) for x in a8]
    q = [_mmb(x, x) for x in p]
    t = [_mmb(eye - x, eye + y) for x, y in zip(a8, p)]
    t = [_mmb(x, eye + y) for x, y in zip(t, q)]
    for n in (8, 16, 32):
        sel = jnp.logical_and(blk(2 * n), jnp.logical_not(blk(n)))
        off = [jnp.where(sel, a, 0.0) for a in mats]
        x = [_mmb(ti, oi) for ti, oi in zip(t, off)]
        t = [ti - _mmb(xi, ti) for ti, xi in zip(t, x)]
    return t


def _stack_heads(x):
    return jnp.concatenate([x[:, h * 128:(h + 1) * 128] for h in range(H_GDN)], axis=0)


def _gdn_kernel(cqkv_ref, z_ref, ab_ref, s0_ref, gn_ref, oa_ref, s_ref, *, nc):
    i = pl.program_id(1)

    @pl.when(i == 0)
    def _():
        s_ref[...] = s0_ref[...]

    n = ROWS_GDN
    row = lax.broadcasted_iota(jnp.int32, (n, n), 0)
    col = lax.broadcasted_iota(jnp.int32, (n, n), 1)
    same = (row >> 6) == (col >> 6)
    lower = jnp.logical_and(same, row >= col)
    strict = jnp.logical_and(same, row > col)
    r64 = lax.broadcasted_iota(jnp.int32, (CHUNK, CHUNK), 0)
    c64 = lax.broadcasted_iota(jnp.int32, (CHUNK, CHUNK), 1)
    tril64 = (r64 >= c64).astype(bf16)
    head_of_row = lax.broadcasted_iota(jnp.int32, (n, DV_GDN), 0) >> 6

    def bcast_col(m, lane):
        return jnp.concatenate(
            [jnp.broadcast_to(m[:, lane + h:lane + h + 1], (m.shape[0], 128)) for h in range(H_GDN)], axis=0)

    chunks = range(nc)
    rows = [slice(c * CHUNK, (c + 1) * CHUNK) for c in chunks]

    gb = [ab_ref[0, r, :] for r in rows]
    parts = [_split3(x) for x in gb]
    gcum = [_mm(tril64, hi) + _mm(tril64, mid) + _mm(tril64, lo) for hi, mid, lo in parts]
    gs = [bcast_col(x, 0) for x in gcum]
    bs = [bcast_col(x, H_GDN) for x in gb]
    gl = [bcast_col(jnp.broadcast_to(x[CHUNK - 1:CHUNK, :], (CHUNK, 128)), 0) for x in gcum]
    grow = [x.T[0:1, :] for x in gs]
    gam = [jnp.where(lower, jnp.exp(jnp.where(lower, jnp.concatenate([x, x], axis=1) - y, 0.0)), 0.0)
           for x, y in zip(gs, grow)]
    ks = [_stack_heads(cqkv_ref[0, r, W_GDN:2 * W_GDN]) for r in rows]
    qs = [_stack_heads(cqkv_ref[0, r, 0:W_GDN]) for r in rows]
    vs = [_stack_heads(cqkv_ref[0, r, 2 * W_GDN:CONV_CH]) for r in rows]
    kq = [_mm_nt(jnp.concatenate([k, q], axis=0).astype(bf16), k.astype(bf16)) for k, q in zip(ks, qs)]
    a = [jnp.where(strict, jnp.concatenate([b_, b_], axis=1) * x[0:n] * g, 0.0) for b_, x, g in zip(bs, kq, gam)]
    qk = [(x[n:2 * n] * g).astype(bf16) for x, g in zip(kq, gam)]
    t = _unit_lower_inverses(a, row, col)
    eg = [jnp.exp(x) for x in gs]
    uw = [_mmb(ti, jnp.concatenate([b_ * v, b_ * e * k], axis=1))
          for ti, b_, v, e, k in zip(t, bs, vs, eg, ks)]
    qg = [q * e for q, e in zip(qs, eg)]
    kdec_t = [(k * jnp.exp(l_ - g)).T.astype(bf16) for k, l_, g in zip(ks, gl, gs)]
    glast = [jnp.exp(x) for x in gl]
    zs = [_stack_heads(z_ref[0, r, :]) for r in rows]
    gate = [x * jax.nn.sigmoid(x) for x in zs]

    for c in chunks:
        u = uw[c][:, 0:DV_GDN]
        w = uw[c][:, DV_GDN:]
        hr = [slice(h * CHUNK, (h + 1) * CHUNK) for h in range(H_GDN)]
        r1 = [_mmb(jnp.concatenate([w[hr[h]], qg[c][hr[h]]], axis=0), s_ref[0, h]) for h in range(H_GDN)]
        vn = jnp.concatenate([u[hr[h]] - r1[h][0:CHUNK] for h in range(H_GDN)], axis=0)
        for h in range(H_GDN):
            vn_h = jnp.where(head_of_row == h, vn, 0.0).astype(bf16)
            decay = jnp.concatenate([glast[c][hr[h]], glast[c][hr[h]]], axis=0)
            s_ref[0, h] = s_ref[0, h] * decay + _mm(kdec_t[c], vn_h)
        o = jnp.concatenate([r1[h][CHUNK:] for h in range(H_GDN)], axis=0) + _mm(qk[c], vn.astype(bf16))
        out = (_rms(o, gn_ref[...]) * gate[c]).astype(bf16)
        for h in range(H_GDN):
            oa_ref[0, rows[c], h * 128:(h + 1) * 128] = out[hr[h]]


def _gdn(cqkv, z, ab, s0, gn):
    b, l, _ = cqkv.shape
    tg = min(l, 256)
    tok = lambda w: pl.BlockSpec((1, tg, w), lambda bi, i: (bi, i, 0))
    sspec = pl.BlockSpec((1, H_GDN, DK_GDN, DV_GDN), lambda bi, i: (bi, 0, 0, 0))
    return pl.pallas_call(
        functools.partial(_gdn_kernel, nc=tg // CHUNK),
        grid=(b, l // tg),
        in_specs=[tok(CONV_CH), tok(W_GDN), tok(128), sspec,
                  pl.BlockSpec((1, DV_GDN), lambda bi, i: (0, 0))],
        out_specs=[tok(W_GDN), sspec],
        out_shape=[jax.ShapeDtypeStruct((b, l, W_GDN), bf16),
                   jax.ShapeDtypeStruct((b, H_GDN, DK_GDN, DV_GDN), f32)],
        compiler_params=_cparams(("arbitrary", "arbitrary")),
        name="gdn",
    )(cqkv, z, ab, s0, gn)


def _attn_kernel(*refs, tq, tk, past, seq, lam_init):
    if past:
        lam_ref, gain_ref, q_ref, k_ref, v_ref, pk_ref, pv_ref, o_ref, vt_ref, acc_ref, m_ref = refs
    else:
        lam_ref, gain_ref, q_ref, k_ref, v_ref, o_ref, vt_ref, acc_ref, m_ref = refs
    i = pl.program_id(1)
    npast = past // tk
    tqe = max(tq, 128)
    heads = range(H_DIFF)
    hs = [slice(h * HD_DIFF, (h + 1) * HD_DIFF) for h in heads]

    def pad_rows(x):
        if x.shape[0] == tk:
            return x
        return jnp.concatenate([x, jnp.zeros((tk - x.shape[0], x.shape[1]), x.dtype)], axis=0)

    @pl.when(i == 0)
    def _build():
        ones = jnp.ones((V_ROWS - HD_DIFF, tk), bf16)

        def put(t, vtile):
            for h in heads:
                vt_ref[t, h, 0:HD_DIFF, :] = vtile[:, hs[h]].astype(f32).T.astype(bf16)
                vt_ref[t, h, HD_DIFF:V_ROWS, :] = ones

        for t in range(npast):
            put(t, pv_ref[0, t * tk:(t + 1) * tk, :])
        for t in range(max(seq // tk, 1)):
            put(npast + t, pad_rows(v_ref[0, t * tk:min((t + 1) * tk, seq), :]))

    q = q_ref[0]
    if tqe > tq:
        q = jnp.concatenate([q, jnp.zeros((tqe - tq, W_DIFF), bf16)], axis=0)
    lane = lax.broadcasted_iota(jnp.int32, (tqe, HD_DIFF), 1)
    zero = jnp.zeros((tqe, HD_DIFF), bf16)
    qps = [jnp.concatenate([jnp.where(lane < D_DIFF, q[:, hs[h]], zero),
                            jnp.where(lane >= D_DIFF, q[:, hs[h]], zero)], axis=0) for h in heads]

    acc_ref[...] = jnp.zeros_like(acc_ref)
    m_ref[...] = jnp.full(m_ref.shape, NEG, f32)

    def tile(j, kts, masked):
        sts = [_mm_nt(kts[h], qps[h]) for h in heads]
        if masked:
            krow = lax.broadcasted_iota(jnp.int32, (tk, 2 * tqe), 0)
            qcol = lax.broadcasted_iota(jnp.int32, (tk, 2 * tqe), 1)
            qcol = jnp.where(qcol >= tqe, qcol - tqe, qcol)
            visible = (krow >> 6) <= (qcol >> 6)
            sts = [jnp.where(visible, st, NEG) for st in sts]
        m_olds = [m_ref[h] for h in heads]
        m_news = [jnp.maximum(m_olds[h], jnp.max(sts[h], axis=0, keepdims=True)) for h in heads]
        alphas = [jnp.exp2(m_olds[h] - m_news[h]) for h in heads]
        ps = [jnp.exp2(sts[h] - m_news[h]).astype(bf16) for h in heads]
        pvs = [_mm(vt_ref[j, h], ps[h]) for h in heads]
        for h in heads:
            acc_ref[h] = alphas[h] * acc_ref[h] + pvs[h]
            m_ref[h] = m_news[h]

    if past:
        nfull = npast

        def full_keys(j):
            rows = pl.ds(pl.multiple_of(j * tk, tk), tk)
            return [pk_ref[0, rows, hs[h]].astype(bf16) for h in heads]

        diag_keys = [pad_rows(k_ref[0, :, hs[h]]) for h in heads]
    else:
        nfull = i

        def full_keys(j):
            rows = pl.ds(pl.multiple_of(j * tk, tk), tk)
            return [k_ref[0, rows, hs[h]] for h in heads]

        diag_keys = full_keys(i)

    def body(j, carry):
        tile(j, full_keys(j), False)
        return carry

    lax.fori_loop(0, nfull, body, 0)
    tile(nfull, diag_keys, True)

    lam = (jnp.exp(jnp.sum(lam_ref[0:1, :] * lam_ref[1:2, :], axis=-1, keepdims=True))
           - jnp.exp(jnp.sum(lam_ref[2:3, :] * lam_ref[3:4, :], axis=-1, keepdims=True)) + lam_init)
    for h in heads:
        acc = acc_ref[h]
        o0 = acc[0:HD_DIFF, 0:tqe] / acc[HD_DIFF:HD_DIFF + 1, 0:tqe]
        o1 = acc[0:HD_DIFF, tqe:] / acc[HD_DIFF:HD_DIFF + 1, tqe:]
        ot = o0 - lam * o1
        ms = jnp.mean(ot * ot, axis=0, keepdims=True)
        o = (ot * lax.rsqrt(ms + EPS)).T * gain_ref[...] * (1.0 - lam_init)
        o_ref[0, :, hs[h]] = o[0:tq].astype(bf16)


def _attn(lamp, gain, qs, kb, vb, pk, pv, lam_init):
    b, l, _ = qs.shape
    tk = ATT_TILE
    tq = min(l, tk)
    past = 0 if pk is None else pk.shape[1]
    assert past % tk == 0 and (l % tk == 0 or l < tk)
    assert past == 0 or l <= tk
    nt = past // tk + max(l // tk, 1)
    tqe = max(tq, 128)
    qspec = pl.BlockSpec((1, tq, W_DIFF), lambda bi, i: (bi, i, 0))
    kspec = pl.BlockSpec((1, l, W_DIFF), lambda bi, i: (bi, 0, 0))
    in_specs = [pl.BlockSpec((8, 128), lambda bi, i: (0, 0)),
                pl.BlockSpec((1, HD_DIFF), lambda bi, i: (0, 0)),
                qspec, kspec, kspec]
    args = [lamp, gain, qs, kb, vb]
    if past:
        pspec = pl.BlockSpec((1, past, W_DIFF), lambda bi, i: (bi, 0, 0))
        in_specs += [pspec, pspec]
        args += [pk, pv]
    return pl.pallas_call(
        functools.partial(_attn_kernel, tq=tq, tk=tk, past=past, seq=l, lam_init=lam_init),
        grid=(b, l // tq),
        in_specs=in_specs,
        out_specs=qspec,
        out_shape=jax.ShapeDtypeStruct((b, l, W_DIFF), bf16),
        scratch_shapes=[pltpu.VMEM((nt, H_DIFF, V_ROWS, tk), bf16),
                        pltpu.VMEM((H_DIFF, V_ROWS, 2 * tqe), f32),
                        pltpu.VMEM((H_DIFF, 1, 2 * tqe), f32)],
        compiler_params=_cparams(("arbitrary", "arbitrary")),
        name="attn",
    )(*args)


def _mix_kernel(x_ref, oa_ref, ob_ref, wo_ref, ln_ref, wq_ref, mk_ref, mv_ref, wmo_ref, h_ref):
    h1 = x_ref[0] + _mm(oa_ref[0], wo_ref[0:W_GDN, :]) + _mm(ob_ref[0], wo_ref[W_GDN:, :])
    hn = _rms(h1, ln_ref[...]).astype(bf16)
    q = (_mm(hn, wq_ref[...]) * (D_MEM ** -0.5 * LOG2E)).astype(bf16)
    parts = []
    for h in range(H_MEM):
        cs = slice(h * D_MEM, (h + 1) * D_MEM)
        s = _mm_nt(q[:, cs], mk_ref[0, :, cs])
        p = jnp.exp2(s - jnp.max(s, axis=-1, keepdims=True))
        den = jnp.sum(p, axis=-1, keepdims=True)
        parts.append(_mm(p.astype(bf16), mv_ref[0, :, cs]) / den)
    o = jnp.concatenate(parts, axis=-1).astype(bf16)
    h_ref[0] = h1 + _mm(o, wmo_ref[...])


def _mix(x, oa, ob, w_out, ln, wq, mk, mv, wmo):
    b, l, _ = x.shape
    tm = min(l, 256)
    n_mem = mk.shape[1]
    tok = lambda w: pl.BlockSpec((1, tm, w), lambda bi, i: (bi, i, 0))
    full = lambda s: pl.BlockSpec(s, lambda bi, i: (0,) * len(s))
    mspec = pl.BlockSpec((1, n_mem, D_MODEL), lambda bi, i: (bi, 0, 0))
    return pl.pallas_call(
        _mix_kernel,
        grid=(b, l // tm),
        in_specs=[tok(D_MODEL), tok(W_GDN), tok(W_DIFF), full((D_MODEL, D_MODEL)), full((1, D_MODEL)),
                  full((D_MODEL, D_MODEL)), mspec, mspec, full((D_MODEL, D_MODEL))],
        out_specs=tok(D_MODEL),
        out_shape=jax.ShapeDtypeStruct((b, l, D_MODEL), f32),
        compiler_params=_cparams(("arbitrary", "arbitrary")),
        name="mix",
    )(x, oa, ob, w_out, ln, wq, mk, mv, wmo)


def _ffn_kernel(h_ref, ln_ref, w1_ref, w2_ref, lnf_ref, o_ref, *, final):
    h = h_ref[...]
    hn = _rms(h, ln_ref[...]).astype(bf16)
    acc = h
    step = 1024
    for c in range(D_FF // step):
        a = jnp.maximum(_mm(hn, w1_ref[:, c * step:(c + 1) * step]), 0.0)
        acc = acc + _mm((a * a).astype(bf16), w2_ref[c * step:(c + 1) * step, :])
    if final:
        acc = _rms(acc, lnf_ref[...])
    o_ref[...] = acc


def _ffn(h2d, ln, w1, w2, lnf, final):
    t = h2d.shape[0]
    tm = min(t, 256)
    full = lambda s: pl.BlockSpec(s, lambda i: (0,) * len(s))
    tok = pl.BlockSpec((tm, D_MODEL), lambda i: (i, 0))
    return pl.pallas_call(
        functools.partial(_ffn_kernel, final=final),
        grid=(t // tm,),
        in_specs=[tok, full((1, D_MODEL)), full((D_MODEL, D_FF)), full((D_FF, D_MODEL)), full((1, D_MODEL))],
        out_specs=tok,
        out_shape=jax.ShapeDtypeStruct((t, D_MODEL), f32),
        compiler_params=_cparams(("arbitrary",)),
        name="ffn",
    )(h2d, ln, w1, w2, lnf)


def _pad_lanes(x, width=128):
    return jnp.pad(x, ((0, 0), (0, width - x.shape[-1])))


def kernel(x_prompt, x_sample, mem_prompt, cache_diff_k, cache_diff_v, cache_mem_k, cache_mem_v,
           state_gdn, state_gdn_conv, ln_mix, w_in, conv_w, a_log, dt_bias, gdn_norm,
           lambda_q1, lambda_k1, lambda_q2, lambda_k2, diff_norm, w_out, ln_mem_q, ln_mem_kv,
           w_mem_q, w_mem_k, w_mem_v, w_mem_o, ln_ffn, w_ff1, w_ff2, ln_final):
    depth = w_in.shape[0]
    bp, lp, _ = x_prompt.shape
    bs, ls, _ = x_sample.shape
    n_mem = mem_prompt.shape[1]
    past = cache_diff_k.shape[2]

    w_main = jnp.concatenate([w_in[:, :, :AB_OFF], w_in[:, :, AB_OFF + 2 * H_GDN:]], axis=2).astype(bf16)
    w_ab = jnp.pad(w_in[:, :, AB_OFF:AB_OFF + 2 * H_GDN], ((0, 0), (0, 0), (0, 128 - 2 * H_GDN))).astype(bf16)
    w_out_b, wq_b, wk_b, wv_b, wmo_b = (w.astype(bf16) for w in (w_out, w_mem_q, w_mem_k, w_mem_v, w_mem_o))
    w1_b, w2_b = w_ff1.astype(bf16), w_ff2.astype(bf16)
    alog_p = _pad_lanes(a_log)
    dtb_p = _pad_lanes(dt_bias)
    lamp = jnp.stack([_pad_lanes(p) for p in (lambda_q1, lambda_k1, lambda_q2, lambda_k2)], axis=1)
    lamp = jnp.pad(lamp, ((0, 0), (0, 4), (0, 0)))
    lnf = ln_final.reshape(1, D_MODEL)

    mk_all, mv_all, mkb_all, mvb_all = _memkv(mem_prompt.reshape(bp * n_mem, D_MODEL),
                                              ln_mem_kv.reshape(depth, 1, D_MODEL), wk_b, wv_b)

    cbuf_p = jnp.zeros((bp, 8, CONV_CH), f32)
    s0_p = jnp.zeros((bp, H_GDN, DK_GDN, DV_GDN), f32)
    cbuf_s_all = jnp.pad(state_gdn_conv, ((0, 0), (0, 0), (8 - (CONV_W - 1), 0), (0, 0)))
    cmk_b = cache_mem_k.reshape(depth, bs, n_mem, D_MODEL).astype(bf16)
    cmv_b = cache_mem_v.reshape(depth, bs, n_mem, D_MODEL).astype(bf16)
    cdk = cache_diff_k.reshape(depth, bs, past, W_DIFF)
    cdv = cache_diff_v.reshape(depth, bs, past, W_DIFF)

    def layer(l, x, cbuf8, s0, pk, pv, mk, mv, final):
        b, ln_, _ = x.shape
        lam_init = 0.8 - 0.6 * math.exp(-0.3 * l)
        cqkv, z, ab, qs, kd, vd, kb, vb, cnew = _inproj(
            x, ln_mix[l].reshape(1, D_MODEL), w_main[l], w_ab[l], conv_w[l],
            alog_p[l:l + 1], dtb_p[l:l + 1], cbuf8)
        oa, s_new = _gdn(cqkv, z, ab, s0, gdn_norm[l].reshape(1, DV_GDN))
        ob = _attn(lamp[l], diff_norm[l].reshape(1, HD_DIFF), qs, kb, vb, pk, pv, lam_init)
        h = _mix(x, oa, ob, w_out_b[l], ln_mem_q[l].reshape(1, D_MODEL), wq_b[l], mk, mv, wmo_b[l])
        h = _ffn(h.reshape(b * ln_, D_MODEL), ln_ffn[l].reshape(1, D_MODEL), w1_b[l], w2_b[l], lnf, final)
        return (h.reshape(b, ln_, D_MODEL), kd.reshape(b, ln_, H_DIFF, HD_DIFF),
                vd.reshape(b, ln_, H_DIFF, HD_DIFF), s_new, cnew[:, 8 - (CONV_W - 1):, :])

    hp, hs = x_prompt, x_sample
    pk_, pv_, ps_, pc_, sk_, sv_, ss_, sc_ = [], [], [], [], [], [], [], []
    for l in range(depth):
        final = l == depth - 1
        hp, k_, v_, s_, c_ = layer(l, hp, cbuf_p, s0_p, None, None,
                                   mkb_all[l].reshape(bp, n_mem, D_MODEL),
                                   mvb_all[l].reshape(bp, n_mem, D_MODEL), final)
        pk_.append(k_); pv_.append(v_); ps_.append(s_); pc_.append(c_)
        hs, k_, v_, s_, c_ = layer(l, hs, cbuf_s_all[l], state_gdn[l], cdk[l], cdv[l],
                                   cmk_b[l], cmv_b[l], final)
        sk_.append(k_); sv_.append(v_); ss_.append(s_); sc_.append(c_)

    mem_shape = (depth, bp, n_mem, H_MEM, D_MEM)
    return (hp, hs,
            jnp.stack(pk_), jnp.stack(pv_), jnp.stack(ps_), jnp.stack(pc_),
            mk_all.reshape(mem_shape), mv_all.reshape(mem_shape),
            jnp.stack(sk_), jnp.stack(sv_), jnp.stack(ss_), jnp.stack(sc_))
```

```python
import functools
import math

import jax
import jax.numpy as jnp
from jax import lax
from jax.experimental import pallas as pl
from jax.experimental.pallas import tpu as pltpu

D_MODEL = 1024
CHUNK = 64
H_GDN = 4
DK_GDN = 128
DV_GDN = 128
CONV_W = 4
W_GDN = H_GDN * DK_GDN
CONV_CH = 3 * W_GDN
H_DIFF = 4
D_DIFF = 64
HD_DIFF = 2 * D_DIFF
W_DIFF = H_DIFF * HD_DIFF
H_MEM = 4
D_MEM = 256
D_FF = 4 * D_MODEL
EPS = 1e-6
LOG2E = 1.4426950408889634
NEG = -1e30
MAIN_COLS = CONV_CH + 4 * 512
AB_OFF = CONV_CH + W_GDN
ROWS_GDN = H_GDN * CHUNK
V_ROWS = HD_DIFF + 16
ATT_TILE = 256
VMEM_LIMIT = 56 * 1024 * 1024

f32 = jnp.float32
bf16 = jnp.bfloat16


def _cparams(sem):
    return pltpu.CompilerParams(dimension_semantics=sem, vmem_limit_bytes=VMEM_LIMIT)


def _rms(x, g):
    ms = jnp.mean(x * x, axis=-1, keepdims=True)
    return x * lax.rsqrt(ms + EPS) * g


def _mm(a, b):
    return jnp.dot(a, b, preferred_element_type=f32)


def _mm_nt(a, b):
    return lax.dot_general(a, b, (((1,), (1,)), ((), ())), preferred_element_type=f32)


def _memkv_kernel(mem_ref, g_ref, wk_ref, wv_ref, mk_ref, mv_ref, mkb_ref, mvb_ref):
    xn = _rms(mem_ref[...], g_ref[0]).astype(bf16)
    k = _mm(xn, wk_ref[0])
    v = _mm(xn, wv_ref[0])
    mk_ref[0] = k
    mv_ref[0] = v
    mkb_ref[0] = k.astype(bf16)
    mvb_ref[0] = v.astype(bf16)


def _memkv(mem2d, ln, wk, wv):
    depth = wk.shape[0]
    t = mem2d.shape[0]
    tm = min(t, 512)
    out = jax.ShapeDtypeStruct((depth, t, D_MODEL), f32)
    outb = jax.ShapeDtypeStruct((depth, t, D_MODEL), bf16)
    wspec = pl.BlockSpec((1, D_MODEL, D_MODEL), lambda l, i: (l, 0, 0))
    ospec = pl.BlockSpec((1, tm, D_MODEL), lambda l, i: (l, i, 0))
    return pl.pallas_call(
        _memkv_kernel,
        grid=(depth, t // tm),
        in_specs=[pl.BlockSpec((tm, D_MODEL), lambda l, i: (i, 0)),
                  pl.BlockSpec((1, 1, D_MODEL), lambda l, i: (l, 0, 0)),
                  wspec, wspec],
        out_specs=[ospec, ospec, ospec, ospec],
        out_shape=[out, out, outb, outb],
        compiler_params=_cparams(("arbitrary", "arbitrary")),
        name="memkv",
    )(mem2d, ln, wk, wv)


def _inproj_kernel(*refs, tm, aliased):
    if aliased:
        refs = refs[:8] + refs[10:]
    (x_ref, ln_ref, w_ref, wab_ref, cw_ref, alog_ref, dtb_ref, cbuf_ref,
     cqkv_ref, z_ref, ab_ref, qs_ref, kd_ref, vd_ref, kb_ref, vb_ref, cnew_ref, cs_ref) = refs
    i = pl.program_id(1)

    @pl.when(i == 0)
    def _():
        cs_ref[0:8, :] = cbuf_ref[0]

    xn = _rms(x_ref[0], ln_ref[...]).astype(bf16)
    conv_in = _mm(xn, w_ref[:, 0:CONV_CH])
    cs_ref[8:8 + tm, :] = conv_in
    for blk in range(CONV_CH // 128):
        cols = slice(blk * 128, (blk + 1) * 128)
        y = cw_ref[0:1, cols] * cs_ref[5:5 + tm, cols]
        y = y + cw_ref[1:2, cols] * cs_ref[6:6 + tm, cols]
        y = y + cw_ref[2:3, cols] * cs_ref[7:7 + tm, cols]
        y = y + cw_ref[3:4, cols] * cs_ref[8:8 + tm, cols]
        c = y * jax.nn.sigmoid(y)
        if blk < 2 * H_GDN:
            c = c * lax.rsqrt(jnp.sum(c * c, axis=-1, keepdims=True) + EPS)
            if blk < H_GDN:
                c = c * DK_GDN ** -0.5
        cqkv_ref[0, :, cols] = c
    tail = cs_ref[tm:tm + 8, :]
    cnew_ref[0] = tail
    cs_ref[0:8, :] = tail

    z_ref[0] = _mm(xn, w_ref[:, CONV_CH:CONV_CH + 512])
    qd = _mm(xn, w_ref[:, CONV_CH + 512:CONV_CH + 1024])
    qs_ref[0] = (qd * (D_DIFF ** -0.5 * LOG2E)).astype(bf16)
    kd = _mm(xn, w_ref[:, CONV_CH + 1024:CONV_CH + 1536])
    kb_ref[0] = kd.astype(bf16)
    vd = _mm(xn, w_ref[:, CONV_CH + 1536:CONV_CH + 2048])
    vb_ref[0] = vd.astype(bf16)
    for h in range(H_DIFF):
        kd_ref[0, 0, pl.ds(h, tm, stride=H_DIFF), :] = kd[:, h * HD_DIFF:(h + 1) * HD_DIFF]
        vd_ref[0, 0, pl.ds(h, tm, stride=H_DIFF), :] = vd[:, h * HD_DIFF:(h + 1) * HD_DIFF]

    ab = _mm(xn, wab_ref[...])
    sp_in = ab + dtb_ref[...]
    softplus = jnp.maximum(sp_in, 0.0) + jnp.log1p(jnp.exp(-jnp.abs(sp_in)))
    g = -jnp.exp(alog_ref[...]) * softplus
    beta = jax.nn.sigmoid(ab)
    lane = lax.broadcasted_iota(jnp.int32, ab.shape, 1)
    ab_ref[0] = jnp.where(lane < H_GDN, g, beta)


def _inproj(x, ln, w_main, w_ab, conv_w, alog, dtb, cbuf8, layer, depth, kd_all, vd_all):
    b, l, _ = x.shape
    tm = min(l, 256)
    nt = l // tm
    aliased = kd_all is not None
    tok = lambda w: pl.BlockSpec((1, tm, w), lambda bi, i: (bi, i, 0))
    full = lambda s: pl.BlockSpec(s, lambda bi, i: (0,) * len(s))
    cache = jax.ShapeDtypeStruct((depth, b, l * H_DIFF, HD_DIFF), f32)
    cache_spec = pl.BlockSpec((1, 1, tm * H_DIFF, HD_DIFF), lambda bi, i: (layer, bi, i, 0))
    out_shape = [
        jax.ShapeDtypeStruct((b, l, CONV_CH), f32),
        jax.ShapeDtypeStruct((b, l, W_GDN), f32),
        jax.ShapeDtypeStruct((b, l, 128), f32),
        jax.ShapeDtypeStruct((b, l, W_DIFF), bf16),
        cache, cache,
        jax.ShapeDtypeStruct((b, l, W_DIFF), bf16),
        jax.ShapeDtypeStruct((b, l, W_DIFF), bf16),
        jax.ShapeDtypeStruct((b, 8, CONV_CH), f32),
    ]
    out_specs = [tok(CONV_CH), tok(W_GDN), tok(128), tok(W_DIFF), cache_spec, cache_spec,
                 tok(W_DIFF), tok(W_DIFF),
                 pl.BlockSpec((1, 8, CONV_CH), lambda bi, i: (bi, 0, 0))]
    in_specs = [tok(D_MODEL), full((1, D_MODEL)), full((D_MODEL, MAIN_COLS)), full((D_MODEL, 128)),
                full((CONV_W, CONV_CH)), full((1, 128)), full((1, 128)),
                pl.BlockSpec((1, 8, CONV_CH), lambda bi, i: (bi, 0, 0))]
    args = [x, ln, w_main, w_ab, conv_w, alog, dtb, cbuf8]
    aliases = {}
    if aliased:
        in_specs += [pl.BlockSpec(memory_space=pl.ANY), pl.BlockSpec(memory_space=pl.ANY)]
        args += [kd_all, vd_all]
        aliases = {8: 4, 9: 5}
    return pl.pallas_call(
        functools.partial(_inproj_kernel, tm=tm, aliased=aliased),
        grid=(b, nt),
        in_specs=in_specs,
        out_specs=out_specs,
        out_shape=out_shape,
        input_output_aliases=aliases,
        scratch_shapes=[pltpu.VMEM((tm + 8, CONV_CH), f32)],
        compiler_params=_cparams(("arbitrary", "arbitrary")),
        name="inproj",
    )(*args)


def _split3(x):
    hi = x.astype(bf16)
    r1 = x - hi.astype(f32)
    mid = r1.astype(bf16)
    lo = (r1 - mid.astype(f32)).astype(bf16)
    return hi, mid, lo


def _mmb(a, b):
    return _mm(a.astype(bf16), b.astype(bf16))


def _unit_lower_inverses(mats, row, col):
    n_rows = row.shape[0]

    def blk(n):
        sh = n.bit_length() - 1
        return (row >> sh) == (col >> sh)

    eye = (row == col).astype(f32)
    a8 = [jnp.where(blk(8), a, 0.0) for a in mats]
    p = [_mmb(x, x) for x in a8]
    q = [_mmb(x, x) for x in p]
    t = [_mmb(eye - x, eye + y) for x, y in zip(a8, p)]
    t = [_mmb(x, eye + y) for x, y in zip(t, q)]
    for n in (8, 16, 32):
        sel = jnp.logical_and(blk(2 * n), jnp.logical_not(blk(n)))
        off = [jnp.where(sel, a, 0.0).astype(bf16) for a in mats]
        starts = range(0, n_rows, 2 * n)
        low = [jnp.concatenate([ti[r + n:r + 2 * n] for r in starts], axis=0) for ti in t]
        x = [_mm(li.astype(bf16), oi) for li, oi in zip(low, off)]
        low = [li - _mmb(xi, ti) for li, xi, ti in zip(low, x, t)]
        t = [jnp.concatenate([piece for k, r in enumerate(starts)
                              for piece in (ti[r:r + n], li[k * n:(k + 1) * n])], axis=0)
             for ti, li in zip(t, low)]
    return t


def _stack_heads(x):
    return jnp.concatenate([x[:, h * 128:(h + 1) * 128] for h in range(H_GDN)], axis=0)


def _gdn_kernel(cqkv_ref, ab_ref, z_ref, s0_ref, gn_ref, oa_ref, s_ref, *, nc, rb):
    i = pl.program_id(1)

    @pl.when(i == 0)
    def _():
        s_ref[...] = s0_ref[...]

    n = ROWS_GDN
    row = lax.broadcasted_iota(jnp.int32, (n, n), 0)
    col = lax.broadcasted_iota(jnp.int32, (n, n), 1)
    same = (row >> 6) == (col >> 6)
    lower = jnp.logical_and(same, row >= col)
    strict = jnp.logical_and(same, row > col)
    r64 = lax.broadcasted_iota(jnp.int32, (CHUNK, CHUNK), 0)
    c64 = lax.broadcasted_iota(jnp.int32, (CHUNK, CHUNK), 1)
    tril64 = (r64 >= c64).astype(bf16)
    head_of_col = lax.broadcasted_iota(jnp.int32, (DK_GDN, n), 1) >> 6
    heads = range(H_GDN)
    hr = [slice(h * CHUNK, (h + 1) * CHUNK) for h in heads]

    def bcast_col(m, lane):
        return jnp.concatenate(
            [jnp.broadcast_to(m[:, lane + h:lane + h + 1], (m.shape[0], 128)) for h in heads], axis=0)

    items = [(r, slice(c * CHUNK, (c + 1) * CHUNK)) for r in range(rb) for c in range(nc)]

    gb = [ab_ref[r, t, :] for r, t in items]
    parts = [_split3(x) for x in gb]
    gcum = [_mm(tril64, hi) + _mm(tril64, mid) + _mm(tril64, lo) for hi, mid, lo in parts]
    gs = [bcast_col(x, 0) for x in gcum]
    bs = [bcast_col(x, H_GDN) for x in gb]
    glast = [x[CHUNK - 1:CHUNK, :] for x in gcum]
    gl = [bcast_col(jnp.broadcast_to(x, (CHUNK, 128)), 0) for x in glast]
    grow = [x.T[0:1, :] for x in gs]
    gam = [jnp.where(lower, jnp.exp(jnp.where(lower, jnp.concatenate([x, x], axis=1) - y, 0.0)), 0.0)
           for x, y in zip(gs, grow)]
    ks = [_stack_heads(cqkv_ref[r, t, W_GDN:2 * W_GDN]) for r, t in items]
    qs = [_stack_heads(cqkv_ref[r, t, 0:W_GDN]) for r, t in items]
    vs = [_stack_heads(cqkv_ref[r, t, 2 * W_GDN:CONV_CH]) for r, t in items]
    kq = [_mm_nt(jnp.concatenate([k, q], axis=0).astype(bf16), k.astype(bf16)) for k, q in zip(ks, qs)]
    a = [jnp.where(strict, jnp.concatenate([b_, b_], axis=1) * x[0:n] * g, 0.0) for b_, x, g in zip(bs, kq, gam)]
    qk = [(x[n:2 * n] * g).astype(bf16) for x, g in zip(kq, gam)]
    t_inv = _unit_lower_inverses(a, row, col)
    eg = [jnp.exp(x) for x in gs]
    uw = [_mmb(ti, jnp.concatenate([b_ * v, b_ * e * k], axis=1))
          for ti, b_, v, e, k in zip(t_inv, bs, vs, eg, ks)]
    wq = [[jnp.concatenate([x[hr[h], DV_GDN:], (q * e)[hr[h]]], axis=0).astype(bf16) for h in heads]
          for x, q, e in zip(uw, qs, eg)]
    kdec_t = [(k * jnp.exp(l_ - g)).T for k, l_, g in zip(ks, gl, gs)]
    kdl = [jnp.concatenate([jnp.where(head_of_col == h, x, 0.0) for h in heads], axis=0).astype(bf16)
           for x in kdec_t]
    decay = [jnp.exp(x) for x in glast]
    decay = [jnp.concatenate([jnp.broadcast_to(x[:, h:h + 1], (DK_GDN, DV_GDN)) for h in heads], axis=0)
             for x in decay]
    gate = [_stack_heads(z_ref[r, t, :]) for r, t in items]
    gate = [x * jax.nn.sigmoid(x) for x in gate]

    s_cur = [s_ref[r].reshape(H_GDN * DK_GDN, DV_GDN) for r in range(rb)]
    for c in range(nc):
        idx = [r * nc + c for r in range(rb)]
        r1 = [[_mm(wq[k][h], s_cur[r][h * DK_GDN:(h + 1) * DK_GDN].astype(bf16)) for h in heads]
              for r, k in enumerate(idx)]
        vn = [(uw[k][:, 0:DV_GDN] - jnp.concatenate([r1[r][h][0:CHUNK] for h in heads], axis=0)).astype(bf16)
              for r, k in enumerate(idx)]
        s_cur = [decay[k] * s_cur[r] + _mm(kdl[k], vn[r]) for r, k in enumerate(idx)]
        o = [jnp.concatenate([r1[r][h][CHUNK:] for h in heads], axis=0) + _mm(qk[k], vn[r])
             for r, k in enumerate(idx)]
        for r, k in enumerate(idx):
            out = (_rms(o[r], gn_ref[...]) * gate[k]).astype(bf16)
            for h in heads:
                oa_ref[r, items[k][1], h * 128:(h + 1) * 128] = out[hr[h]]
    for r in range(rb):
        s_ref[r] = s_cur[r].reshape(H_GDN, DK_GDN, DV_GDN)


def _gdn(cqkv, z, ab, s0, gn):
    b, l, _ = cqkv.shape
    tg = min(l, 256)
    rb = 2 if b % 2 == 0 else 1
    tok = lambda w: pl.BlockSpec((rb, tg, w), lambda bi, i: (bi, i, 0))
    sspec = pl.BlockSpec((rb, H_GDN, DK_GDN, DV_GDN), lambda bi, i: (bi, 0, 0, 0))
    return pl.pallas_call(
        functools.partial(_gdn_kernel, nc=tg // CHUNK, rb=rb),
        grid=(b // rb, l // tg),
        in_specs=[tok(CONV_CH), tok(128), tok(W_GDN), sspec,
                  pl.BlockSpec((1, DV_GDN), lambda bi, i: (0, 0))],
        out_specs=[tok(W_GDN), sspec],
        out_shape=[jax.ShapeDtypeStruct((b, l, W_GDN), bf16),
                   jax.ShapeDtypeStruct((b, H_GDN, DK_GDN, DV_GDN), f32)],
        compiler_params=_cparams(("arbitrary", "arbitrary")),
        name="gdn",
    )(cqkv, ab, z, s0, gn)


def _attn_kernel(*refs, tq, tk, past, seq, lam_init):
    if past:
        lam_ref, gain_ref, q_ref, k_ref, v_ref, pk_ref, pv_ref, o_ref, vt_ref, acc_ref, m_ref = refs
    else:
        lam_ref, gain_ref, q_ref, k_ref, v_ref, o_ref, vt_ref, acc_ref, m_ref = refs
    i = pl.program_id(1)
    npast = past // tk
    tqe = max(tq, 128)
    heads = range(H_DIFF)
    hs = [slice(h * HD_DIFF, (h + 1) * HD_DIFF) for h in heads]

    def pad_rows(x):
        if x.shape[0] == tk:
            return x
        return jnp.concatenate([x, jnp.zeros((tk - x.shape[0], x.shape[1]), x.dtype)], axis=0)

    def history(ref, t, h):
        return ref[0, 0, pl.ds(t * tk * H_DIFF + h, tk, stride=H_DIFF), :]

    @pl.when(i == 0)
    def _build():
        ones = jnp.ones((V_ROWS - HD_DIFF, tk), bf16)

        def put(t, h, vtile):
            vt_ref[t, h, 0:HD_DIFF, :] = vtile.astype(f32).T.astype(bf16)
            vt_ref[t, h, HD_DIFF:V_ROWS, :] = ones

        for t in range(npast):
            for h in heads:
                put(t, h, history(pv_ref, t, h))
        for t in range(max(seq // tk, 1)):
            vtile = pad_rows(v_ref[0, t * tk:min((t + 1) * tk, seq), :])
            for h in heads:
                put(npast + t, h, vtile[:, hs[h]])

    q = q_ref[0]
    if tqe > tq:
        q = jnp.concatenate([q, jnp.zeros((tqe - tq, W_DIFF), bf16)], axis=0)
    lane = lax.broadcasted_iota(jnp.int32, (tqe, HD_DIFF), 1)
    zero = jnp.zeros((tqe, HD_DIFF), bf16)
    qps = [jnp.concatenate([jnp.where(lane < D_DIFF, q[:, hs[h]], zero),
                            jnp.where(lane >= D_DIFF, q[:, hs[h]], zero)], axis=0) for h in heads]

    acc_ref[...] = jnp.zeros_like(acc_ref)
    m_ref[...] = jnp.full(m_ref.shape, NEG, f32)

    def scores(kts):
        return [_mm_nt(kts[h], qps[h]) for h in heads]

    def process(tiles):
        sts = []
        for _, st, masked in tiles:
            if masked:
                krow = lax.broadcasted_iota(jnp.int32, (tk, 2 * tqe), 0)
                qcol = lax.broadcasted_iota(jnp.int32, (tk, 2 * tqe), 1)
                qcol = jnp.where(qcol >= tqe, qcol - tqe, qcol)
                visible = (krow >> 6) <= (qcol >> 6)
                st = [jnp.where(visible, x, NEG) for x in st]
            sts.append(st)
        m_olds = [m_ref[h] for h in heads]
        m_news = m_olds
        for st in sts:
            m_news = [jnp.maximum(m_news[h], jnp.max(st[h], axis=0, keepdims=True)) for h in heads]
        alphas = [jnp.exp2(m_olds[h] - m_news[h]) for h in heads]
        pvs = None
        for (j, _, _), st in zip(tiles, sts):
            ps = [jnp.exp2(st[h] - m_news[h]).astype(bf16) for h in heads]
            pv = [_mm(vt_ref[j, h], ps[h]) for h in heads]
            pvs = pv if pvs is None else [pvs[h] + pv[h] for h in heads]
        for h in heads:
            acc_ref[h] = alphas[h] * acc_ref[h] + pvs[h]
            m_ref[h] = m_news[h]

    if past:
        diag_keys = [pad_rows(k_ref[0, :, hs[h]]) for h in heads]
        keys = [[history(pk_ref, t, h).astype(bf16) for h in heads] for t in range(npast)] + [diag_keys]
        for j in range(0, npast + 1, 2):
            process([(t, scores(keys[t]), t == npast) for t in range(j, min(j + 2, npast + 1))])
    else:
        def keys(j):
            rows = pl.ds(pl.multiple_of(j * tk, tk), tk)
            return [k_ref[0, rows, hs[h]] for h in heads]

        def body(jj, carry):
            j = 2 * jj
            process([(j, scores(keys(j)), False), (j + 1, scores(keys(j + 1)), False)])
            return carry

        lax.fori_loop(0, i // 2, body, 0)
        odd = lax.rem(i, 2) == 1

        @pl.when(odd)
        def _():
            process([(i - 1, scores(keys(i - 1)), False), (i, scores(keys(i)), True)])

        @pl.when(jnp.logical_not(odd))
        def _():
            process([(i, scores(keys(i)), True)])

    lam = (jnp.exp(jnp.sum(lam_ref[0:1, :] * lam_ref[1:2, :], axis=-1, keepdims=True))
           - jnp.exp(jnp.sum(lam_ref[2:3, :] * lam_ref[3:4, :], axis=-1, keepdims=True)) + lam_init)
    for h in heads:
        acc = acc_ref[h]
        o0 = acc[0:HD_DIFF, 0:tqe] / acc[HD_DIFF:HD_DIFF + 1, 0:tqe]
        o1 = acc[0:HD_DIFF, tqe:] / acc[HD_DIFF:HD_DIFF + 1, tqe:]
        ot = o0 - lam * o1
        ms = jnp.mean(ot * ot, axis=0, keepdims=True)
        o = (ot * lax.rsqrt(ms + EPS)).T * gain_ref[...] * (1.0 - lam_init)
        o_ref[0, :, hs[h]] = o[0:tq].astype(bf16)


def _attn(lamp, gain, qs, kb, vb, hist_k, hist_v, layer, lam_init):
    b, l, _ = qs.shape
    tk = ATT_TILE
    tq = min(l, tk)
    past = 0 if hist_k is None else hist_k.shape[2] // H_DIFF
    assert past % tk == 0 and (l % tk == 0 or l < tk)
    assert past == 0 or l <= tk
    nt = past // tk + max(l // tk, 1)
    tqe = max(tq, 128)
    qspec = pl.BlockSpec((1, tq, W_DIFF), lambda bi, i: (bi, i, 0))
    kspec = pl.BlockSpec((1, l, W_DIFF), lambda bi, i: (bi, 0, 0))
    in_specs = [pl.BlockSpec((8, 128), lambda bi, i: (0, 0)),
                pl.BlockSpec((1, HD_DIFF), lambda bi, i: (0, 0)),
                qspec, kspec, kspec]
    args = [lamp, gain, qs, kb, vb]
    scratch = [pltpu.VMEM((nt, H_DIFF, V_ROWS, tk), bf16),
               pltpu.VMEM((H_DIFF, V_ROWS, 2 * tqe), f32),
               pltpu.VMEM((H_DIFF, 1, 2 * tqe), f32)]
    if past:
        pspec = pl.BlockSpec((1, 1, past * H_DIFF, HD_DIFF), lambda bi, i: (layer, bi, 0, 0))
        in_specs += [pspec, pspec]
        args += [hist_k, hist_v]
    return pl.pallas_call(
        functools.partial(_attn_kernel, tq=tq, tk=tk, past=past, seq=l, lam_init=lam_init),
        grid=(b, l // tq),
        in_specs=in_specs,
        out_specs=qspec,
        out_shape=jax.ShapeDtypeStruct((b, l, W_DIFF), bf16),
        scratch_shapes=scratch,
        compiler_params=_cparams(("arbitrary", "arbitrary")),
        name="attn",
    )(*args)


def _mix_kernel(x_ref, oa_ref, ob_ref, wo_ref, ln_ref, wq_ref, mk_ref, mv_ref, wmo_ref, h_ref):
    h1 = x_ref[0] + _mm(oa_ref[0], wo_ref[0:W_GDN, :]) + _mm(ob_ref[0], wo_ref[W_GDN:, :])
    hn = _rms(h1, ln_ref[...]).astype(bf16)
    q = (_mm(hn, wq_ref[...]) * (D_MEM ** -0.5 * LOG2E)).astype(bf16)
    parts = []
    for h in range(H_MEM):
        cs = slice(h * D_MEM, (h + 1) * D_MEM)
        s = _mm_nt(q[:, cs], mk_ref[0, :, cs])
        p = jnp.exp2(s - jnp.max(s, axis=-1, keepdims=True))
        den = jnp.sum(p, axis=-1, keepdims=True)
        parts.append(_mm(p.astype(bf16), mv_ref[0, :, cs]) / den)
    o = jnp.concatenate(parts, axis=-1).astype(bf16)
    h_ref[0] = h1 + _mm(o, wmo_ref[...])


def _mix(x, oa, ob, w_out, ln, wq, mk, mv, wmo):
    b, l, _ = x.shape
    tm = min(l, 256)
    n_mem = mk.shape[1]
    tok = lambda w: pl.BlockSpec((1, tm, w), lambda bi, i: (bi, i, 0))
    full = lambda s: pl.BlockSpec(s, lambda bi, i: (0,) * len(s))
    mspec = pl.BlockSpec((1, n_mem, D_MODEL), lambda bi, i: (bi, 0, 0))
    return pl.pallas_call(
        _mix_kernel,
        grid=(b, l // tm),
        in_specs=[tok(D_MODEL), tok(W_GDN), tok(W_DIFF), full((D_MODEL, D_MODEL)), full((1, D_MODEL)),
                  full((D_MODEL, D_MODEL)), mspec, mspec, full((D_MODEL, D_MODEL))],
        out_specs=tok(D_MODEL),
        out_shape=jax.ShapeDtypeStruct((b, l, D_MODEL), f32),
        compiler_params=_cparams(("arbitrary", "arbitrary")),
        name="mix",
    )(x, oa, ob, w_out, ln, wq, mk, mv, wmo)


def _ffn_kernel(h_ref, ln_ref, w1_ref, w2_ref, lnf_ref, o_ref, *, final):
    h = h_ref[...]
    hn = _rms(h, ln_ref[...]).astype(bf16)
    acc = h
    step = 1024
    for c in range(D_FF // step):
        a = jnp.maximum(_mm(hn, w1_ref[:, c * step:(c + 1) * step]), 0.0)
        acc = acc + _mm((a * a).astype(bf16), w2_ref[c * step:(c + 1) * step, :])
    if final:
        acc = _rms(acc, lnf_ref[...])
    o_ref[...] = acc


def _ffn(h2d, ln, w1, w2, lnf, final):
    t = h2d.shape[0]
    tm = min(t, 256)
    full = lambda s: pl.BlockSpec(s, lambda i: (0,) * len(s))
    tok = pl.BlockSpec((tm, D_MODEL), lambda i: (i, 0))
    return pl.pallas_call(
        functools.partial(_ffn_kernel, final=final),
        grid=(t // tm,),
        in_specs=[tok, full((1, D_MODEL)), full((D_MODEL, D_FF)), full((D_FF, D_MODEL)), full((1, D_MODEL))],
        out_specs=tok,
        out_shape=jax.ShapeDtypeStruct((t, D_MODEL), f32),
        compiler_params=_cparams(("arbitrary",)),
        name="ffn",
    )(h2d, ln, w1, w2, lnf)


def _pad_lanes(x, width=128):
    return jnp.pad(x, ((0, 0), (0, width - x.shape[-1])))


def kernel(x_prompt, x_sample, mem_prompt, cache_diff_k, cache_diff_v, cache_mem_k, cache_mem_v,
           state_gdn, state_gdn_conv, ln_mix, w_in, conv_w, a_log, dt_bias, gdn_norm,
           lambda_q1, lambda_k1, lambda_q2, lambda_k2, diff_norm, w_out, ln_mem_q, ln_mem_kv,
           w_mem_q, w_mem_k, w_mem_v, w_mem_o, ln_ffn, w_ff1, w_ff2, ln_final):
    depth = w_in.shape[0]
    bp, lp, _ = x_prompt.shape
    bs, ls, _ = x_sample.shape
    n_mem = mem_prompt.shape[1]
    past = cache_diff_k.shape[2]

    w_main = jnp.concatenate([w_in[:, :, :AB_OFF], w_in[:, :, AB_OFF + 2 * H_GDN:]], axis=2).astype(bf16)
    w_ab = jnp.pad(w_in[:, :, AB_OFF:AB_OFF + 2 * H_GDN], ((0, 0), (0, 0), (0, 128 - 2 * H_GDN))).astype(bf16)
    w_out_b, wq_b, wk_b, wv_b, wmo_b = (w.astype(bf16) for w in (w_out, w_mem_q, w_mem_k, w_mem_v, w_mem_o))
    w1_b, w2_b = w_ff1.astype(bf16), w_ff2.astype(bf16)
    alog_p = _pad_lanes(a_log)
    dtb_p = _pad_lanes(dt_bias)
    lamp = jnp.stack([_pad_lanes(p) for p in (lambda_q1, lambda_k1, lambda_q2, lambda_k2)], axis=1)
    lamp = jnp.pad(lamp, ((0, 0), (0, 4), (0, 0)))
    lnf = ln_final.reshape(1, D_MODEL)

    mk_all, mv_all, mkb_all, mvb_all = _memkv(mem_prompt.reshape(bp * n_mem, D_MODEL),
                                              ln_mem_kv.reshape(depth, 1, D_MODEL), wk_b, wv_b)

    cbuf_p = jnp.zeros((bp, 8, CONV_CH), f32)
    s0_p = jnp.zeros((bp, H_GDN, DK_GDN, DV_GDN), f32)
    cbuf_s_all = jnp.pad(state_gdn_conv, ((0, 0), (0, 0), (8 - (CONV_W - 1), 0), (0, 0)))
    cmk_b = cache_mem_k.reshape(depth, bs, n_mem, D_MODEL).astype(bf16)
    cmv_b = cache_mem_v.reshape(depth, bs, n_mem, D_MODEL).astype(bf16)
    hist_k = cache_diff_k.reshape(depth, bs, past * H_DIFF, HD_DIFF)
    hist_v = cache_diff_v.reshape(depth, bs, past * H_DIFF, HD_DIFF)

    def layer(l, x, cbuf8, s0, hk, hv, mk, mv, kd_all, vd_all, final):
        b, ln_, _ = x.shape
        lam_init = 0.8 - 0.6 * math.exp(-0.3 * l)
        cqkv, z, ab, qs, kd_all, vd_all, kb, vb, cnew = _inproj(
            x, ln_mix[l].reshape(1, D_MODEL), w_main[l], w_ab[l], conv_w[l],
            alog_p[l:l + 1], dtb_p[l:l + 1], cbuf8, l, depth, kd_all, vd_all)
        oa, s_new = _gdn(cqkv, z, ab, s0, gdn_norm[l].reshape(1, DV_GDN))
        ob = _attn(lamp[l], diff_norm[l].reshape(1, HD_DIFF), qs, kb, vb, hk, hv, l, lam_init)
        h = _mix(x, oa, ob, w_out_b[l], ln_mem_q[l].reshape(1, D_MODEL), wq_b[l], mk, mv, wmo_b[l])
        h = _ffn(h.reshape(b * ln_, D_MODEL), ln_ffn[l].reshape(1, D_MODEL), w1_b[l], w2_b[l], lnf, final)
        return h.reshape(b, ln_, D_MODEL), kd_all, vd_all, s_new, cnew[:, 8 - (CONV_W - 1):, :]

    hp, hs = x_prompt, x_sample
    pk_ = pv_ = sk_ = sv_ = None
    ps_, pc_, ss_, sc_ = [], [], [], []
    for l in range(depth):
        final = l == depth - 1
        hp, pk_, pv_, s_, c_ = layer(l, hp, cbuf_p, s0_p, None, None,
                                     mkb_all[l].reshape(bp, n_mem, D_MODEL),
                                     mvb_all[l].reshape(bp, n_mem, D_MODEL), pk_, pv_, final)
        ps_.append(s_); pc_.append(c_)
        hs, sk_, sv_, s_, c_ = layer(l, hs, cbuf_s_all[l], state_gdn[l], hist_k, hist_v,
                                     cmk_b[l], cmv_b[l], sk_, sv_, final)
        ss_.append(s_); sc_.append(c_)

    mem_shape = (depth, bp, n_mem, H_MEM, D_MEM)
    return (hp, hs,
            pk_.reshape(depth, bp, lp, H_DIFF, HD_DIFF), pv_.reshape(depth, bp, lp, H_DIFF, HD_DIFF),
            jnp.stack(ps_), jnp.stack(pc_),
            mk_all.reshape(mem_shape), mv_all.reshape(mem_shape),
            sk_.reshape(depth, bs, ls, H_DIFF, HD_DIFF), sv_.reshape(depth, bs, ls, H_DIFF, HD_DIFF),
            jnp.stack(ss_), jnp.stack(sc_))
```

```python
import functools
import math

import jax
import jax.numpy as jnp
from jax import lax
from jax.experimental import pallas as pl
from jax.experimental.pallas import tpu as pltpu

D_MODEL = 1024
CHUNK = 64
H_GDN = 4
DK_GDN = 128
DV_GDN = 128
CONV_W = 4
W_GDN = H_GDN * DK_GDN
CONV_CH = 3 * W_GDN
H_DIFF = 4
D_DIFF = 64
HD_DIFF = 2 * D_DIFF
W_DIFF = H_DIFF * HD_DIFF
H_MEM = 4
D_MEM = 256
D_FF = 4 * D_MODEL
EPS = 1e-6
LOG2E = 1.4426950408889634
NEG = -1e30
MAIN_COLS = CONV_CH + 4 * 512
AB_OFF = CONV_CH + W_GDN
ROWS_GDN = H_GDN * CHUNK
V_ROWS = HD_DIFF + 16
ATT_TILE = 256
VMEM_LIMIT = 56 * 1024 * 1024

f32 = jnp.float32
bf16 = jnp.bfloat16


def _cparams(sem):
    return pltpu.CompilerParams(dimension_semantics=sem, vmem_limit_bytes=VMEM_LIMIT)


def _rms(x, g):
    ms = jnp.mean(x * x, axis=-1, keepdims=True)
    return x * lax.rsqrt(ms + EPS) * g


def _mm(a, b):
    return jnp.dot(a, b, preferred_element_type=f32)


def _mm_nt(a, b):
    return lax.dot_general(a, b, (((1,), (1,)), ((), ())), preferred_element_type=f32)


def _memkv_kernel(mem_ref, g_ref, wk_ref, wv_ref, mk_ref, mv_ref, mkb_ref, mvb_ref):
    xn = _rms(mem_ref[...], g_ref[0]).astype(bf16)
    k = _mm(xn, wk_ref[0])
    v = _mm(xn, wv_ref[0])
    mk_ref[0] = k
    mv_ref[0] = v
    mkb_ref[0] = k.astype(bf16)
    mvb_ref[0] = v.astype(bf16)


def _memkv(mem2d, ln, wk, wv):
    depth = wk.shape[0]
    t = mem2d.shape[0]
    tm = min(t, 512)
    out = jax.ShapeDtypeStruct((depth, t, D_MODEL), f32)
    outb = jax.ShapeDtypeStruct((depth, t, D_MODEL), bf16)
    wspec = pl.BlockSpec((1, D_MODEL, D_MODEL), lambda l, i: (l, 0, 0))
    ospec = pl.BlockSpec((1, tm, D_MODEL), lambda l, i: (l, i, 0))
    return pl.pallas_call(
        _memkv_kernel,
        grid=(depth, t // tm),
        in_specs=[pl.BlockSpec((tm, D_MODEL), lambda l, i: (i, 0)),
                  pl.BlockSpec((1, 1, D_MODEL), lambda l, i: (l, 0, 0)),
                  wspec, wspec],
        out_specs=[ospec, ospec, ospec, ospec],
        out_shape=[out, out, outb, outb],
        compiler_params=_cparams(("arbitrary", "arbitrary")),
        name="memkv",
    )(mem2d, ln, wk, wv)


def _inproj_kernel(*refs, tm, aliased):
    if aliased:
        refs = refs[:8] + refs[10:]
    (x_ref, ln_ref, w_ref, wab_ref, cw_ref, alog_ref, dtb_ref, cbuf_ref,
     cqkv_ref, z_ref, ab_ref, qs_ref, kd_ref, vd_ref, kb_ref, vb_ref, cnew_ref, cs_ref) = refs
    i = pl.program_id(1)

    @pl.when(i == 0)
    def _():
        cs_ref[0:8, :] = cbuf_ref[0]

    xn = _rms(x_ref[0], ln_ref[...]).astype(bf16)
    conv_in = _mm(xn, w_ref[:, 0:CONV_CH])
    cs_ref[8:8 + tm, :] = conv_in
    for blk in range(CONV_CH // 128):
        cols = slice(blk * 128, (blk + 1) * 128)
        y = cw_ref[0:1, cols] * cs_ref[5:5 + tm, cols]
        y = y + cw_ref[1:2, cols] * cs_ref[6:6 + tm, cols]
        y = y + cw_ref[2:3, cols] * cs_ref[7:7 + tm, cols]
        y = y + cw_ref[3:4, cols] * cs_ref[8:8 + tm, cols]
        c = y * jax.nn.sigmoid(y)
        if blk < 2 * H_GDN:
            c = c * lax.rsqrt(jnp.sum(c * c, axis=-1, keepdims=True) + EPS)
            if blk < H_GDN:
                c = c * DK_GDN ** -0.5
        cqkv_ref[0, :, cols] = c
    tail = cs_ref[tm:tm + 8, :]
    cnew_ref[0] = tail
    cs_ref[0:8, :] = tail

    z_ref[0] = _mm(xn, w_ref[:, CONV_CH:CONV_CH + 512])
    qd = _mm(xn, w_ref[:, CONV_CH + 512:CONV_CH + 1024])
    qs_ref[0] = (qd * (D_DIFF ** -0.5 * LOG2E)).astype(bf16)
    kd = _mm(xn, w_ref[:, CONV_CH + 1024:CONV_CH + 1536])
    kb_ref[0] = kd.astype(bf16)
    vd = _mm(xn, w_ref[:, CONV_CH + 1536:CONV_CH + 2048])
    vb_ref[0] = vd.astype(bf16)
    for h in range(H_DIFF):
        kd_ref[0, 0, pl.ds(h, tm, stride=H_DIFF), :] = kd[:, h * HD_DIFF:(h + 1) * HD_DIFF]
        vd_ref[0, 0, pl.ds(h, tm, stride=H_DIFF), :] = vd[:, h * HD_DIFF:(h + 1) * HD_DIFF]

    ab = _mm(xn, wab_ref[...])
    sp_in = ab + dtb_ref[...]
    softplus = jnp.maximum(sp_in, 0.0) + jnp.log1p(jnp.exp(-jnp.abs(sp_in)))
    g = -jnp.exp(alog_ref[...]) * softplus
    beta = jax.nn.sigmoid(ab)
    lane = lax.broadcasted_iota(jnp.int32, ab.shape, 1)
    ab_ref[0] = jnp.where(lane < H_GDN, g, beta)


def _inproj(x, ln, w_main, w_ab, conv_w, alog, dtb, cbuf8, layer, depth, kd_all, vd_all):
    b, l, _ = x.shape
    tm = min(l, 512)
    nt = l // tm
    aliased = kd_all is not None
    tok = lambda w: pl.BlockSpec((1, tm, w), lambda bi, i: (bi, i, 0))
    full = lambda s: pl.BlockSpec(s, lambda bi, i: (0,) * len(s), pipeline_mode=pl.Buffered(1))
    cache = jax.ShapeDtypeStruct((depth, b, l * H_DIFF, HD_DIFF), f32)
    cache_spec = pl.BlockSpec((1, 1, tm * H_DIFF, HD_DIFF), lambda bi, i: (layer, bi, i, 0))
    out_shape = [
        jax.ShapeDtypeStruct((b, l, CONV_CH), f32),
        jax.ShapeDtypeStruct((b, l, W_GDN), f32),
        jax.ShapeDtypeStruct((b, l, 128), f32),
        jax.ShapeDtypeStruct((b, l, W_DIFF), bf16),
        cache, cache,
        jax.ShapeDtypeStruct((b, l, W_DIFF), bf16),
        jax.ShapeDtypeStruct((b, l, W_DIFF), bf16),
        jax.ShapeDtypeStruct((b, 8, CONV_CH), f32),
    ]
    out_specs = [tok(CONV_CH), tok(W_GDN), tok(128), tok(W_DIFF), cache_spec, cache_spec,
                 tok(W_DIFF), tok(W_DIFF),
                 pl.BlockSpec((1, 8, CONV_CH), lambda bi, i: (bi, 0, 0))]
    in_specs = [tok(D_MODEL), full((1, D_MODEL)), full((D_MODEL, MAIN_COLS)), full((D_MODEL, 128)),
                full((CONV_W, CONV_CH)), full((1, 128)), full((1, 128)),
                pl.BlockSpec((1, 8, CONV_CH), lambda bi, i: (bi, 0, 0))]
    args = [x, ln, w_main, w_ab, conv_w, alog, dtb, cbuf8]
    aliases = {}
    if aliased:
        in_specs += [pl.BlockSpec(memory_space=pl.ANY), pl.BlockSpec(memory_space=pl.ANY)]
        args += [kd_all, vd_all]
        aliases = {8: 4, 9: 5}
    return pl.pallas_call(
        functools.partial(_inproj_kernel, tm=tm, aliased=aliased),
        grid=(b, nt),
        in_specs=in_specs,
        out_specs=out_specs,
        out_shape=out_shape,
        input_output_aliases=aliases,
        scratch_shapes=[pltpu.VMEM((tm + 8, CONV_CH), f32)],
        compiler_params=_cparams(("arbitrary", "arbitrary")),
        name="inproj",
    )(*args)


def _split3(x):
    hi = x.astype(bf16)
    r1 = x - hi.astype(f32)
    mid = r1.astype(bf16)
    lo = (r1 - mid.astype(f32)).astype(bf16)
    return hi, mid, lo


def _mmb(a, b):
    return _mm(a.astype(bf16), b.astype(bf16))


def _unit_lower_inverses(mats, row, col):
    n_rows = row.shape[0]

    def blk(n):
        sh = n.bit_length() - 1
        return (row >> sh) == (col >> sh)

    eye = (row == col).astype(f32)
    a8 = [jnp.where(blk(8), a, 0.0) for a in mats]
    p = [_mmb(x, x) for x in a8]
    q = [_mmb(x, x) for x in p]
    t = [_mmb(eye - x, eye + y) for x, y in zip(a8, p)]
    t = [_mmb(x, eye + y) for x, y in zip(t, q)]
    for n in (8, 16, 32):
        sel = jnp.logical_and(blk(2 * n), jnp.logical_not(blk(n)))
        off = [jnp.where(sel, a, 0.0).astype(bf16) for a in mats]
        starts = range(0, n_rows, 2 * n)
        low = [jnp.concatenate([ti[r + n:r + 2 * n] for r in starts], axis=0) for ti in t]
        x = [_mm(li.astype(bf16), oi) for li, oi in zip(low, off)]
        low = [li - _mmb(xi, ti) for li, xi, ti in zip(low, x, t)]
        t = [jnp.concatenate([piece for k, r in enumerate(starts)
                              for piece in (ti[r:r + n], li[k * n:(k + 1) * n])], axis=0)
             for ti, li in zip(t, low)]
    return t


def _stack_heads(x):
    return jnp.concatenate([x[:, h * 128:(h + 1) * 128] for h in range(H_GDN)], axis=0)


def _gdn_kernel(cqkv_ref, ab_ref, z_ref, s0_ref, gn_ref, oa_ref, s_ref, *, nc, rb):
    i = pl.program_id(1)

    @pl.when(i == 0)
    def _():
        s_ref[...] = s0_ref[...]

    n = ROWS_GDN
    row = lax.broadcasted_iota(jnp.int32, (n, n), 0)
    col = lax.broadcasted_iota(jnp.int32, (n, n), 1)
    same = (row >> 6) == (col >> 6)
    lower = jnp.logical_and(same, row >= col)
    strict = jnp.logical_and(same, row > col)
    r64 = lax.broadcasted_iota(jnp.int32, (CHUNK, CHUNK), 0)
    c64 = lax.broadcasted_iota(jnp.int32, (CHUNK, CHUNK), 1)
    tril64 = (r64 >= c64).astype(bf16)
    head_of_col = lax.broadcasted_iota(jnp.int32, (DK_GDN, n), 1) >> 6
    heads = range(H_GDN)
    hr = [slice(h * CHUNK, (h + 1) * CHUNK) for h in heads]

    def bcast_col(m, lane):
        return jnp.concatenate(
            [jnp.broadcast_to(m[:, lane + h:lane + h + 1], (m.shape[0], 128)) for h in heads], axis=0)

    items = [(r, slice(c * CHUNK, (c + 1) * CHUNK)) for r in range(rb) for c in range(nc)]

    gb = [ab_ref[r, t, :] for r, t in items]
    parts = [_split3(x) for x in gb]
    gcum = [_mm(tril64, hi) + _mm(tril64, mid) + _mm(tril64, lo) for hi, mid, lo in parts]
    gs = [bcast_col(x, 0) for x in gcum]
    bs = [bcast_col(x, H_GDN) for x in gb]
    glast = [x[CHUNK - 1:CHUNK, :] for x in gcum]
    gl = [bcast_col(jnp.broadcast_to(x, (CHUNK, 128)), 0) for x in glast]
    grow = [x.T[0:1, :] for x in gs]
    gam = [jnp.where(lower, jnp.exp(jnp.where(lower, jnp.concatenate([x, x], axis=1) - y, 0.0)), 0.0)
           for x, y in zip(gs, grow)]
    ks = [_stack_heads(cqkv_ref[r, t, W_GDN:2 * W_GDN]) for r, t in items]
    qs = [_stack_heads(cqkv_ref[r, t, 0:W_GDN]) for r, t in items]
    vs = [_stack_heads(cqkv_ref[r, t, 2 * W_GDN:CONV_CH]) for r, t in items]
    kq = [_mm_nt(jnp.concatenate([k, q], axis=0).astype(bf16), k.astype(bf16)) for k, q in zip(ks, qs)]
    a = [jnp.where(strict, jnp.concatenate([b_, b_], axis=1) * x[0:n] * g, 0.0) for b_, x, g in zip(bs, kq, gam)]
    qk = [(x[n:2 * n] * g).astype(bf16) for x, g in zip(kq, gam)]
    t_inv = _unit_lower_inverses(a, row, col)
    eg = [jnp.exp(x) for x in gs]
    uw = [_mmb(ti, jnp.concatenate([b_ * v, b_ * e * k], axis=1))
          for ti, b_, v, e, k in zip(t_inv, bs, vs, eg, ks)]
    wq = [[jnp.concatenate([x[hr[h], DV_GDN:], (q * e)[hr[h]]], axis=0).astype(bf16) for h in heads]
          for x, q, e in zip(uw, qs, eg)]
    kdec_t = [(k * jnp.exp(l_ - g)).T for k, l_, g in zip(ks, gl, gs)]
    kdl = [jnp.concatenate([jnp.where(head_of_col == h, x, 0.0) for h in heads], axis=0).astype(bf16)
           for x in kdec_t]
    decay = [jnp.exp(x) for x in glast]
    decay = [jnp.concatenate([jnp.broadcast_to(x[:, h:h + 1], (DK_GDN, DV_GDN)) for h in heads], axis=0)
             for x in decay]
    gate = [_stack_heads(z_ref[r, t, :]) for r, t in items]
    gate = [x * jax.nn.sigmoid(x) for x in gate]

    s_cur = [s_ref[r].reshape(H_GDN * DK_GDN, DV_GDN) for r in range(rb)]
    for c in range(nc):
        idx = [r * nc + c for r in range(rb)]
        r1 = [[_mm(wq[k][h], s_cur[r][h * DK_GDN:(h + 1) * DK_GDN].astype(bf16)) for h in heads]
              for r, k in enumerate(idx)]
        vn = [(uw[k][:, 0:DV_GDN] - jnp.concatenate([r1[r][h][0:CHUNK] for h in heads], axis=0)).astype(bf16)
              for r, k in enumerate(idx)]
        s_cur = [decay[k] * s_cur[r] + _mm(kdl[k], vn[r]) for r, k in enumerate(idx)]
        o = [jnp.concatenate([r1[r][h][CHUNK:] for h in heads], axis=0) + _mm(qk[k], vn[r])
             for r, k in enumerate(idx)]
        for r, k in enumerate(idx):
            out = (_rms(o[r], gn_ref[...]) * gate[k]).astype(bf16)
            for h in heads:
                oa_ref[r, items[k][1], h * 128:(h + 1) * 128] = out[hr[h]]
    for r in range(rb):
        s_ref[r] = s_cur[r].reshape(H_GDN, DK_GDN, DV_GDN)


def _gdn(cqkv, z, ab, s0, gn):
    b, l, _ = cqkv.shape
    tg = min(l, 256)
    rb = 2 if b % 2 == 0 else 1
    tok = lambda w: pl.BlockSpec((rb, tg, w), lambda bi, i: (bi, i, 0))
    sspec = pl.BlockSpec((rb, H_GDN, DK_GDN, DV_GDN), lambda bi, i: (bi, 0, 0, 0))
    return pl.pallas_call(
        functools.partial(_gdn_kernel, nc=tg // CHUNK, rb=rb),
        grid=(b // rb, l // tg),
        in_specs=[tok(CONV_CH), tok(128), tok(W_GDN), sspec,
                  pl.BlockSpec((1, DV_GDN), lambda bi, i: (0, 0))],
        out_specs=[tok(W_GDN), sspec],
        out_shape=[jax.ShapeDtypeStruct((b, l, W_GDN), bf16),
                   jax.ShapeDtypeStruct((b, H_GDN, DK_GDN, DV_GDN), f32)],
        compiler_params=_cparams(("arbitrary", "arbitrary")),
        name="gdn",
    )(cqkv, ab, z, s0, gn)


def _attn_kernel(*refs, tq, tk, past, seq, rb, lam_init):
    if past:
        lam_ref, gain_ref, q_ref, k_ref, v_ref, pk_ref, pv_ref, o_ref, vt_ref, acc_ref, m_ref = refs
    else:
        lam_ref, gain_ref, q_ref, k_ref, v_ref, o_ref, vt_ref, acc_ref, m_ref = refs
    i = pl.program_id(1)
    npast = past // tk
    tqe = max(tq, 128)
    heads = range(rb * H_DIFF)
    row_of = [u // H_DIFF for u in heads]
    head_of = [u % H_DIFF for u in heads]
    hs = [slice(head_of[u] * HD_DIFF, (head_of[u] + 1) * HD_DIFF) for u in heads]

    def pad_rows(x):
        if x.shape[0] == tk:
            return x
        return jnp.concatenate([x, jnp.zeros((tk - x.shape[0], x.shape[1]), x.dtype)], axis=0)

    def history(ref, t, u):
        return ref[0, row_of[u], pl.ds(t * tk * H_DIFF + head_of[u], tk, stride=H_DIFF), :]

    @pl.when(i == 0)
    def _build():
        ones = jnp.ones((V_ROWS - HD_DIFF, tk), bf16)

        def put(t, u, vtile):
            vt_ref[t, u, 0:HD_DIFF, :] = vtile.astype(f32).T.astype(bf16)
            vt_ref[t, u, HD_DIFF:V_ROWS, :] = ones

        for t in range(npast):
            for u in heads:
                put(t, u, history(pv_ref, t, u))
        for t in range(max(seq // tk, 1)):
            for u in heads:
                put(npast + t, u, pad_rows(v_ref[row_of[u], t * tk:min((t + 1) * tk, seq), hs[u]]))

    def padded_q(u):
        qu = q_ref[row_of[u], :, hs[u]]
        if tqe > tq:
            qu = jnp.concatenate([qu, jnp.zeros((tqe - tq, HD_DIFF), bf16)], axis=0)
        return qu

    lane = lax.broadcasted_iota(jnp.int32, (tqe, HD_DIFF), 1)
    zero = jnp.zeros((tqe, HD_DIFF), bf16)
    qps = [jnp.concatenate([jnp.where(lane < D_DIFF, padded_q(u), zero),
                            jnp.where(lane >= D_DIFF, padded_q(u), zero)], axis=0) for u in heads]

    acc_ref[...] = jnp.zeros_like(acc_ref)
    m_ref[...] = jnp.full(m_ref.shape, NEG, f32)

    def scores(kts):
        return [_mm_nt(kts[h], qps[h]) for h in heads]

    def process(tiles):
        sts = []
        for _, st, masked in tiles:
            if masked:
                krow = lax.broadcasted_iota(jnp.int32, (tk, 2 * tqe), 0)
                qcol = lax.broadcasted_iota(jnp.int32, (tk, 2 * tqe), 1)
                qcol = jnp.where(qcol >= tqe, qcol - tqe, qcol)
                visible = (krow >> 6) <= (qcol >> 6)
                st = [jnp.where(visible, x, NEG) for x in st]
            sts.append(st)
        m_olds = [m_ref[h] for h in heads]
        m_news = m_olds
        for st in sts:
            m_news = [jnp.maximum(m_news[h], jnp.max(st[h], axis=0, keepdims=True)) for h in heads]
        alphas = [jnp.exp2(m_olds[h] - m_news[h]) for h in heads]
        pvs = None
        for (j, _, _), st in zip(tiles, sts):
            ps = [jnp.exp2(st[h] - m_news[h]).astype(bf16) for h in heads]
            pv = [_mm(vt_ref[j, h], ps[h]) for h in heads]
            pvs = pv if pvs is None else [pvs[h] + pv[h] for h in heads]
        for h in heads:
            acc_ref[h] = alphas[h] * acc_ref[h] + pvs[h]
            m_ref[h] = m_news[h]

    if past:
        diag_keys = [pad_rows(k_ref[row_of[u], :, hs[u]]) for u in heads]
        keys = [[history(pk_ref, t, u).astype(bf16) for u in heads] for t in range(npast)] + [diag_keys]
        for j in range(0, npast + 1, 2):
            process([(t, scores(keys[t]), t == npast) for t in range(j, min(j + 2, npast + 1))])
    else:
        def keys(j):
            rows = pl.ds(pl.multiple_of(j * tk, tk), tk)
            return [k_ref[row_of[u], rows, hs[u]] for u in heads]

        def body(jj, carry):
            j = 2 * jj
            process([(j, scores(keys(j)), False), (j + 1, scores(keys(j + 1)), False)])
            return carry

        lax.fori_loop(0, i // 2, body, 0)
        odd = lax.rem(i, 2) == 1

        @pl.when(odd)
        def _():
            process([(i - 1, scores(keys(i - 1)), False), (i, scores(keys(i)), True)])

        @pl.when(jnp.logical_not(odd))
        def _():
            process([(i, scores(keys(i)), True)])

    lam = (jnp.exp(jnp.sum(lam_ref[0:1, :] * lam_ref[1:2, :], axis=-1, keepdims=True))
           - jnp.exp(jnp.sum(lam_ref[2:3, :] * lam_ref[3:4, :], axis=-1, keepdims=True)) + lam_init)
    for h in heads:
        acc = acc_ref[h]
        o0 = acc[0:HD_DIFF, 0:tqe] / acc[HD_DIFF:HD_DIFF + 1, 0:tqe]
        o1 = acc[0:HD_DIFF, tqe:] / acc[HD_DIFF:HD_DIFF + 1, tqe:]
        ot = o0 - lam * o1
        ms = jnp.mean(ot * ot, axis=0, keepdims=True)
        o = (ot * lax.rsqrt(ms + EPS)).T * gain_ref[...] * (1.0 - lam_init)
        o_ref[row_of[h], :, hs[h]] = o[0:tq].astype(bf16)


def _attn(lamp, gain, qs, kb, vb, hist_k, hist_v, layer, lam_init):
    b, l, _ = qs.shape
    tk = ATT_TILE
    tq = min(l, tk)
    rb = 2 if b % 2 == 0 else 1
    past = 0 if hist_k is None else hist_k.shape[2] // H_DIFF
    assert past % tk == 0 and (l % tk == 0 or l < tk)
    assert past == 0 or l <= tk
    nt = past // tk + max(l // tk, 1)
    tqe = max(tq, 128)
    qspec = pl.BlockSpec((rb, tq, W_DIFF), lambda bi, i: (bi, i, 0))
    kspec = pl.BlockSpec((rb, l, W_DIFF), lambda bi, i: (bi, 0, 0), pipeline_mode=pl.Buffered(1))
    in_specs = [pl.BlockSpec((8, 128), lambda bi, i: (0, 0)),
                pl.BlockSpec((1, HD_DIFF), lambda bi, i: (0, 0)),
                qspec, kspec, kspec]
    args = [lamp, gain, qs, kb, vb]
    scratch = [pltpu.VMEM((nt, rb * H_DIFF, V_ROWS, tk), bf16),
               pltpu.VMEM((rb * H_DIFF, V_ROWS, 2 * tqe), f32),
               pltpu.VMEM((rb * H_DIFF, 1, 2 * tqe), f32)]
    if past:
        pspec = pl.BlockSpec((1, rb, past * H_DIFF, HD_DIFF), lambda bi, i: (layer, bi, 0, 0),
                             pipeline_mode=pl.Buffered(1))
        in_specs += [pspec, pspec]
        args += [hist_k, hist_v]
    return pl.pallas_call(
        functools.partial(_attn_kernel, tq=tq, tk=tk, past=past, seq=l, rb=rb, lam_init=lam_init),
        grid=(b // rb, l // tq),
        in_specs=in_specs,
        out_specs=qspec,
        out_shape=jax.ShapeDtypeStruct((b, l, W_DIFF), bf16),
        scratch_shapes=scratch,
        compiler_params=_cparams(("arbitrary", "arbitrary")),
        name="attn",
    )(*args)


def _mix_kernel(x_ref, oa_ref, ob_ref, wo_ref, ln_ref, wq_ref, mk_ref, mv_ref, wmo_ref, h_ref):
    h1 = x_ref[0] + _mm(oa_ref[0], wo_ref[0:W_GDN, :]) + _mm(ob_ref[0], wo_ref[W_GDN:, :])
    hn = _rms(h1, ln_ref[...]).astype(bf16)
    q = (_mm(hn, wq_ref[...]) * (D_MEM ** -0.5 * LOG2E)).astype(bf16)
    parts = []
    for h in range(H_MEM):
        cs = slice(h * D_MEM, (h + 1) * D_MEM)
        s = _mm_nt(q[:, cs], mk_ref[0, :, cs])
        p = jnp.exp2(s - jnp.max(s, axis=-1, keepdims=True))
        den = jnp.sum(p, axis=-1, keepdims=True)
        parts.append(_mm(p.astype(bf16), mv_ref[0, :, cs]) / den)
    o = jnp.concatenate(parts, axis=-1).astype(bf16)
    h_ref[0] = h1 + _mm(o, wmo_ref[...])


def _mix(x, oa, ob, w_out, ln, wq, mk, mv, wmo):
    b, l, _ = x.shape
    tm = min(l, 512)
    n_mem = mk.shape[1]
    tok = lambda w: pl.BlockSpec((1, tm, w), lambda bi, i: (bi, i, 0))
    full = lambda s: pl.BlockSpec(s, lambda bi, i: (0,) * len(s), pipeline_mode=pl.Buffered(1))
    mspec = pl.BlockSpec((1, n_mem, D_MODEL), lambda bi, i: (bi, 0, 0))
    return pl.pallas_call(
        _mix_kernel,
        grid=(b, l // tm),
        in_specs=[tok(D_MODEL), tok(W_GDN), tok(W_DIFF), full((D_MODEL, D_MODEL)), full((1, D_MODEL)),
                  full((D_MODEL, D_MODEL)), mspec, mspec, full((D_MODEL, D_MODEL))],
        out_specs=tok(D_MODEL),
        out_shape=jax.ShapeDtypeStruct((b, l, D_MODEL), f32),
        compiler_params=_cparams(("arbitrary", "arbitrary")),
        name="mix",
    )(x, oa, ob, w_out, ln, wq, mk, mv, wmo)


def _ffn_kernel(h_ref, ln_ref, w1_ref, w2_ref, lnf_ref, o_ref, *, final):
    h = h_ref[...]
    hn = _rms(h, ln_ref[...]).astype(bf16)
    acc = h
    step = 1024
    for c in range(D_FF // step):
        a = jnp.maximum(_mm(hn, w1_ref[:, c * step:(c + 1) * step]), 0.0)
        acc = acc + _mm((a * a).astype(bf16), w2_ref[c * step:(c + 1) * step, :])
    if final:
        acc = _rms(acc, lnf_ref[...])
    o_ref[...] = acc


def _ffn(h2d, ln, w1, w2, lnf, final):
    t = h2d.shape[0]
    tm = min(t, 512)
    full = lambda s: pl.BlockSpec(s, lambda i: (0,) * len(s), pipeline_mode=pl.Buffered(1))
    tok = pl.BlockSpec((tm, D_MODEL), lambda i: (i, 0))
    return pl.pallas_call(
        functools.partial(_ffn_kernel, final=final),
        grid=(t // tm,),
        in_specs=[tok, full((1, D_MODEL)), full((D_MODEL, D_FF)), full((D_FF, D_MODEL)), full((1, D_MODEL))],
        out_specs=tok,
        out_shape=jax.ShapeDtypeStruct((t, D_MODEL), f32),
        compiler_params=_cparams(("arbitrary",)),
        name="ffn",
    )(h2d, ln, w1, w2, lnf)


def _pad_lanes(x, width=128):
    return jnp.pad(x, ((0, 0), (0, width - x.shape[-1])))


def kernel(x_prompt, x_sample, mem_prompt, cache_diff_k, cache_diff_v, cache_mem_k, cache_mem_v,
           state_gdn, state_gdn_conv, ln_mix, w_in, conv_w, a_log, dt_bias, gdn_norm,
           lambda_q1, lambda_k1, lambda_q2, lambda_k2, diff_norm, w_out, ln_mem_q, ln_mem_kv,
           w_mem_q, w_mem_k, w_mem_v, w_mem_o, ln_ffn, w_ff1, w_ff2, ln_final):
    depth = w_in.shape[0]
    bp, lp, _ = x_prompt.shape
    bs, ls, _ = x_sample.shape
    n_mem = mem_prompt.shape[1]
    past = cache_diff_k.shape[2]

    w_main = jnp.concatenate([w_in[:, :, :AB_OFF], w_in[:, :, AB_OFF + 2 * H_GDN:]], axis=2).astype(bf16)
    w_ab = jnp.pad(w_in[:, :, AB_OFF:AB_OFF + 2 * H_GDN], ((0, 0), (0, 0), (0, 128 - 2 * H_GDN))).astype(bf16)
    w_out_b, wq_b, wk_b, wv_b, wmo_b = (w.astype(bf16) for w in (w_out, w_mem_q, w_mem_k, w_mem_v, w_mem_o))
    w1_b, w2_b = w_ff1.astype(bf16), w_ff2.astype(bf16)
    alog_p = _pad_lanes(a_log)
    dtb_p = _pad_lanes(dt_bias)
    lamp = jnp.stack([_pad_lanes(p) for p in (lambda_q1, lambda_k1, lambda_q2, lambda_k2)], axis=1)
    lamp = jnp.pad(lamp, ((0, 0), (0, 4), (0, 0)))
    lnf = ln_final.reshape(1, D_MODEL)

    mk_all, mv_all, mkb_all, mvb_all = _memkv(mem_prompt.reshape(bp * n_mem, D_MODEL),
                                              ln_mem_kv.reshape(depth, 1, D_MODEL), wk_b, wv_b)

    cbuf_p = jnp.zeros((bp, 8, CONV_CH), f32)
    s0_p = jnp.zeros((bp, H_GDN, DK_GDN, DV_GDN), f32)
    cbuf_s_all = jnp.pad(state_gdn_conv, ((0, 0), (0, 0), (8 - (CONV_W - 1), 0), (0, 0)))
    cmk_b = cache_mem_k.reshape(depth, bs, n_mem, D_MODEL).astype(bf16)
    cmv_b = cache_mem_v.reshape(depth, bs, n_mem, D_MODEL).astype(bf16)
    hist_k = cache_diff_k.reshape(depth, bs, past * H_DIFF, HD_DIFF)
    hist_v = cache_diff_v.reshape(depth, bs, past * H_DIFF, HD_DIFF)

    def layer(l, x, cbuf8, s0, hk, hv, mk, mv, kd_all, vd_all, final):
        b, ln_, _ = x.shape
        lam_init = 0.8 - 0.6 * math.exp(-0.3 * l)
        cqkv, z, ab, qs, kd_all, vd_all, kb, vb, cnew = _inproj(
            x, ln_mix[l].reshape(1, D_MODEL), w_main[l], w_ab[l], conv_w[l],
            alog_p[l:l + 1], dtb_p[l:l + 1], cbuf8, l, depth, kd_all, vd_all)
        oa, s_new = _gdn(cqkv, z, ab, s0, gdn_norm[l].reshape(1, DV_GDN))
        ob = _attn(lamp[l], diff_norm[l].reshape(1, HD_DIFF), qs, kb, vb, hk, hv, l, lam_init)
        h = _mix(x, oa, ob, w_out_b[l], ln_mem_q[l].reshape(1, D_MODEL), wq_b[l], mk, mv, wmo_b[l])
        h = _ffn(h.reshape(b * ln_, D_MODEL), ln_ffn[l].reshape(1, D_MODEL), w1_b[l], w2_b[l], lnf, final)
        return h.reshape(b, ln_, D_MODEL), kd_all, vd_all, s_new, cnew[:, 8 - (CONV_W - 1):, :]

    hp, hs = x_prompt, x_sample
    pk_ = pv_ = sk_ = sv_ = None
    ps_, pc_, ss_, sc_ = [], [], [], []
    for l in range(depth):
        final = l == depth - 1
        hp, pk_, pv_, s_, c_ = layer(l, hp, cbuf_p, s0_p, None, None,
                                     mkb_all[l].reshape(bp, n_mem, D_MODEL),
                                     mvb_all[l].reshape(bp, n_mem, D_MODEL), pk_, pv_, final)
        ps_.append(s_); pc_.append(c_)
        hs, sk_, sv_, s_, c_ = layer(l, hs, cbuf_s_all[l], state_gdn[l], hist_k, hist_v,
                                     cmk_b[l], cmv_b[l], sk_, sv_, final)
        ss_.append(s_); sc_.append(c_)

    mem_shape = (depth, bp, n_mem, H_MEM, D_MEM)
    return (hp, hs,
            pk_.reshape(depth, bp, lp, H_DIFF, HD_DIFF), pv_.reshape(depth, bp, lp, H_DIFF, HD_DIFF),
            jnp.stack(ps_), jnp.stack(pc_),
            mk_all.reshape(mem_shape), mv_all.reshape(mem_shape),
            sk_.reshape(depth, bs, ls, H_DIFF, HD_DIFF), sv_.reshape(depth, bs, ls, H_DIFF, HD_DIFF),
            jnp.stack(ss_), jnp.stack(sc_))
```

```python
import functools
import math

import jax
import jax.numpy as jnp
from jax import lax
from jax.experimental import pallas as pl
from jax.experimental.pallas import tpu as pltpu

D_MODEL = 1024
CHUNK = 64
H_GDN = 4
DK_GDN = 128
DV_GDN = 128
CONV_W = 4
W_GDN = H_GDN * DK_GDN
CONV_CH = 3 * W_GDN
H_DIFF = 4
D_DIFF = 64
HD_DIFF = 2 * D_DIFF
W_DIFF = H_DIFF * HD_DIFF
H_MEM = 4
D_MEM = 256
D_FF = 4 * D_MODEL
EPS = 1e-6
LOG2E = 1.4426950408889634
NEG = -1e30
MAIN_COLS = CONV_CH + 4 * 512
AB_OFF = CONV_CH + W_GDN
ROWS_GDN = H_GDN * CHUNK
V_ROWS = HD_DIFF + 16
ATT_TILE = 256
VMEM_LIMIT = 56 * 1024 * 1024

f32 = jnp.float32
bf16 = jnp.bfloat16


def _cparams(sem):
    return pltpu.CompilerParams(dimension_semantics=sem, vmem_limit_bytes=VMEM_LIMIT)


def _rms(x, g):
    ms = jnp.mean(x * x, axis=-1, keepdims=True)
    return x * lax.rsqrt(ms + EPS) * g


def _mm(a, b):
    return jnp.dot(a, b, preferred_element_type=f32)


def _mm_nt(a, b):
    return lax.dot_general(a, b, (((1,), (1,)), ((), ())), preferred_element_type=f32)


def _memkv_kernel(mem_ref, g_ref, wk_ref, wv_ref, mk_ref, mv_ref, mkb_ref, mvb_ref):
    xn = _rms(mem_ref[...], g_ref[0]).astype(bf16)
    k = _mm(xn, wk_ref[0])
    v = _mm(xn, wv_ref[0])
    mk_ref[0] = k
    mv_ref[0] = v
    mkb_ref[0] = k.astype(bf16)
    mvb_ref[0] = v.astype(bf16)


def _memkv(mem2d, ln, wk, wv):
    depth = wk.shape[0]
    t = mem2d.shape[0]
    tm = min(t, 512)
    out = jax.ShapeDtypeStruct((depth, t, D_MODEL), f32)
    outb = jax.ShapeDtypeStruct((depth, t, D_MODEL), bf16)
    wspec = pl.BlockSpec((1, D_MODEL, D_MODEL), lambda l, i: (l, 0, 0))
    ospec = pl.BlockSpec((1, tm, D_MODEL), lambda l, i: (l, i, 0))
    return pl.pallas_call(
        _memkv_kernel,
        grid=(depth, t // tm),
        in_specs=[pl.BlockSpec((tm, D_MODEL), lambda l, i: (i, 0)),
                  pl.BlockSpec((1, 1, D_MODEL), lambda l, i: (l, 0, 0)),
                  wspec, wspec],
        out_specs=[ospec, ospec, ospec, ospec],
        out_shape=[out, out, outb, outb],
        compiler_params=_cparams(("arbitrary", "arbitrary")),
        name="memkv",
    )(mem2d, ln, wk, wv)


def _inproj_kernel(*refs, tm, aliased):
    if aliased:
        refs = refs[:8] + refs[10:]
    (x_ref, ln_ref, w_ref, wab_ref, cw_ref, alog_ref, dtb_ref, cbuf_ref,
     cqkv_ref, z_ref, ab_ref, qs_ref, kd_ref, vd_ref, kb_ref, vb_ref, cnew_ref, cs_ref) = refs
    i = pl.program_id(1)

    @pl.when(i == 0)
    def _():
        cs_ref[0:8, :] = cbuf_ref[0]

    xn = _rms(x_ref[0], ln_ref[...]).astype(bf16)
    conv_in = _mm(xn, w_ref[:, 0:CONV_CH])
    cs_ref[8:8 + tm, :] = conv_in
    for blk in range(CONV_CH // 128):
        cols = slice(blk * 128, (blk + 1) * 128)
        y = cw_ref[0:1, cols] * cs_ref[5:5 + tm, cols]
        y = y + cw_ref[1:2, cols] * cs_ref[6:6 + tm, cols]
        y = y + cw_ref[2:3, cols] * cs_ref[7:7 + tm, cols]
        y = y + cw_ref[3:4, cols] * cs_ref[8:8 + tm, cols]
        c = y * jax.nn.sigmoid(y)
        if blk < 2 * H_GDN:
            c = c * lax.rsqrt(jnp.sum(c * c, axis=-1, keepdims=True) + EPS)
            if blk < H_GDN:
                c = c * DK_GDN ** -0.5
        cqkv_ref[0, :, cols] = c
    tail = cs_ref[tm:tm + 8, :]
    cnew_ref[0] = tail
    cs_ref[0:8, :] = tail

    z_ref[0] = _mm(xn, w_ref[:, CONV_CH:CONV_CH + 512])
    qd = _mm(xn, w_ref[:, CONV_CH + 512:CONV_CH + 1024])
    qs_ref[0] = (qd * (D_DIFF ** -0.5 * LOG2E)).astype(bf16)
    kd = _mm(xn, w_ref[:, CONV_CH + 1024:CONV_CH + 1536])
    kb_ref[0] = kd.astype(bf16)
    vd = _mm(xn, w_ref[:, CONV_CH + 1536:CONV_CH + 2048])
    vb_ref[0] = vd.astype(bf16)
    for h in range(H_DIFF):
        kd_ref[0, 0, pl.ds(h, tm, stride=H_DIFF), :] = kd[:, h * HD_DIFF:(h + 1) * HD_DIFF]
        vd_ref[0, 0, pl.ds(h, tm, stride=H_DIFF), :] = vd[:, h * HD_DIFF:(h + 1) * HD_DIFF]

    ab = _mm(xn, wab_ref[...])
    sp_in = ab + dtb_ref[...]
    softplus = jnp.maximum(sp_in, 0.0) + jnp.log1p(jnp.exp(-jnp.abs(sp_in)))
    g = -jnp.exp(alog_ref[...]) * softplus
    beta = jax.nn.sigmoid(ab)
    lane = lax.broadcasted_iota(jnp.int32, ab.shape, 1)
    ab_ref[0] = jnp.where(lane < H_GDN, g, beta)


def _inproj(x, ln, w_main, w_ab, conv_w, alog, dtb, cbuf8, layer, depth, kd_all, vd_all):
    b, l, _ = x.shape
    tm = min(l, 512)
    nt = l // tm
    aliased = kd_all is not None
    tok = lambda w: pl.BlockSpec((1, tm, w), lambda bi, i: (bi, i, 0))
    full = lambda s: pl.BlockSpec(s, lambda bi, i: (0,) * len(s), pipeline_mode=pl.Buffered(1))
    cache = jax.ShapeDtypeStruct((depth, b, l * H_DIFF, HD_DIFF), f32)
    cache_spec = pl.BlockSpec((1, 1, tm * H_DIFF, HD_DIFF), lambda bi, i: (layer, bi, i, 0))
    out_shape = [
        jax.ShapeDtypeStruct((b, l, CONV_CH), f32),
        jax.ShapeDtypeStruct((b, l, W_GDN), f32),
        jax.ShapeDtypeStruct((b, l, 128), f32),
        jax.ShapeDtypeStruct((b, l, W_DIFF), bf16),
        cache, cache,
        jax.ShapeDtypeStruct((b, l, W_DIFF), bf16),
        jax.ShapeDtypeStruct((b, l, W_DIFF), bf16),
        jax.ShapeDtypeStruct((b, 8, CONV_CH), f32),
    ]
    out_specs = [tok(CONV_CH), tok(W_GDN), tok(128), tok(W_DIFF), cache_spec, cache_spec,
                 tok(W_DIFF), tok(W_DIFF),
                 pl.BlockSpec((1, 8, CONV_CH), lambda bi, i: (bi, 0, 0))]
    in_specs = [tok(D_MODEL), full((1, D_MODEL)), full((D_MODEL, MAIN_COLS)), full((D_MODEL, 128)),
                full((CONV_W, CONV_CH)), full((1, 128)), full((1, 128)),
                pl.BlockSpec((1, 8, CONV_CH), lambda bi, i: (bi, 0, 0))]
    args = [x, ln, w_main, w_ab, conv_w, alog, dtb, cbuf8]
    aliases = {}
    if aliased:
        in_specs += [pl.BlockSpec(memory_space=pl.ANY), pl.BlockSpec(memory_space=pl.ANY)]
        args += [kd_all, vd_all]
        aliases = {8: 4, 9: 5}
    return pl.pallas_call(
        functools.partial(_inproj_kernel, tm=tm, aliased=aliased),
        grid=(b, nt),
        in_specs=in_specs,
        out_specs=out_specs,
        out_shape=out_shape,
        input_output_aliases=aliases,
        scratch_shapes=[pltpu.VMEM((tm + 8, CONV_CH), f32)],
        compiler_params=_cparams(("arbitrary", "arbitrary")),
        name="inproj",
    )(*args)


def _split3(x):
    hi = x.astype(bf16)
    r1 = x - hi.astype(f32)
    mid = r1.astype(bf16)
    lo = (r1 - mid.astype(f32)).astype(bf16)
    return hi, mid, lo


def _mmb(a, b):
    return _mm(a.astype(bf16), b.astype(bf16))


def _unit_lower_inverses(mats, row, col):
    n_rows = row.shape[0]

    def blk(n):
        sh = n.bit_length() - 1
        return (row >> sh) == (col >> sh)

    eye = (row == col).astype(f32)
    a8 = [jnp.where(blk(8), a, 0.0) for a in mats]
    p = [_mmb(x, x) for x in a8]
    q = [_mmb(x, x) for x in p]
    t = [_mmb(eye - x, eye + y) for x, y in zip(a8, p)]
    t = [_mmb(x, eye + y) for x, y in zip(t, q)]
    for n in (8, 16, 32):
        sel = jnp.logical_and(blk(2 * n), jnp.logical_not(blk(n)))
        off = [jnp.where(sel, a, 0.0).astype(bf16) for a in mats]
        starts = range(0, n_rows, 2 * n)
        low = [jnp.concatenate([ti[r + n:r + 2 * n] for r in starts], axis=0) for ti in t]
        x = [_mm(li.astype(bf16), oi) for li, oi in zip(low, off)]
        low = [li - _mmb(xi, ti) for li, xi, ti in zip(low, x, t)]
        t = [jnp.concatenate([piece for k, r in enumerate(starts)
                              for piece in (ti[r:r + n], li[k * n:(k + 1) * n])], axis=0)
             for ti, li in zip(t, low)]
    return t


def _stack_heads(x):
    return jnp.concatenate([x[:, h * 128:(h + 1) * 128] for h in range(H_GDN)], axis=0)


def _gdn_kernel(cqkv_ref, ab_ref, z_ref, s0_ref, gn_ref, oa_ref, s_ref, *, nc, rb):
    i = pl.program_id(1)

    @pl.when(i == 0)
    def _():
        s_ref[...] = s0_ref[...]

    n = ROWS_GDN
    row = lax.broadcasted_iota(jnp.int32, (n, n), 0)
    col = lax.broadcasted_iota(jnp.int32, (n, n), 1)
    same = (row >> 6) == (col >> 6)
    lower = jnp.logical_and(same, row >= col)
    strict = jnp.logical_and(same, row > col)
    r64 = lax.broadcasted_iota(jnp.int32, (CHUNK, CHUNK), 0)
    c64 = lax.broadcasted_iota(jnp.int32, (CHUNK, CHUNK), 1)
    tril64 = (r64 >= c64).astype(bf16)
    head_of_col = lax.broadcasted_iota(jnp.int32, (DK_GDN, n), 1) >> 6
    heads = range(H_GDN)
    hr = [slice(h * CHUNK, (h + 1) * CHUNK) for h in heads]

    def bcast_col(m, lane):
        return jnp.concatenate(
            [jnp.broadcast_to(m[:, lane + h:lane + h + 1], (m.shape[0], 128)) for h in heads], axis=0)

    items = [(r, slice(c * CHUNK, (c + 1) * CHUNK)) for r in range(rb) for c in range(nc)]

    gb = [ab_ref[r, t, :] for r, t in items]
    parts = [_split3(x) for x in gb]
    gcum = [_mm(tril64, hi) + _mm(tril64, mid) + _mm(tril64, lo) for hi, mid, lo in parts]
    gs = [bcast_col(x, 0) for x in gcum]
    bs = [bcast_col(x, H_GDN) for x in gb]
    glast = [x[CHUNK - 1:CHUNK, :] for x in gcum]
    gl = [bcast_col(jnp.broadcast_to(x, (CHUNK, 128)), 0) for x in glast]
    grow = [x.T[0:1, :] for x in gs]
    gam = [jnp.where(lower, jnp.exp(jnp.where(lower, jnp.concatenate([x, x], axis=1) - y, 0.0)), 0.0)
           for x, y in zip(gs, grow)]
    ks = [_stack_heads(cqkv_ref[r, t, W_GDN:2 * W_GDN]) for r, t in items]
    qs = [_stack_heads(cqkv_ref[r, t, 0:W_GDN]) for r, t in items]
    vs = [_stack_heads(cqkv_ref[r, t, 2 * W_GDN:CONV_CH]) for r, t in items]
    kq = [_mm_nt(jnp.concatenate([k, q], axis=0).astype(bf16), k.astype(bf16)) for k, q in zip(ks, qs)]
    a = [jnp.where(strict, jnp.concatenate([b_, b_], axis=1) * x[0:n] * g, 0.0) for b_, x, g in zip(bs, kq, gam)]
    qk = [(x[n:2 * n] * g).astype(bf16) for x, g in zip(kq, gam)]
    t_inv = _unit_lower_inverses(a, row, col)
    eg = [jnp.exp(x) for x in gs]
    uw = [_mmb(ti, jnp.concatenate([b_ * v, b_ * e * k], axis=1))
          for ti, b_, v, e, k in zip(t_inv, bs, vs, eg, ks)]
    wq = [[jnp.concatenate([x[hr[h], DV_GDN:], (q * e)[hr[h]]], axis=0).astype(bf16) for h in heads]
          for x, q, e in zip(uw, qs, eg)]
    kdec_t = [(k * jnp.exp(l_ - g)).T for k, l_, g in zip(ks, gl, gs)]
    kdl = [jnp.concatenate([jnp.where(head_of_col == h, x, 0.0) for h in heads], axis=0).astype(bf16)
           for x in kdec_t]
    decay = [jnp.exp(x) for x in glast]
    decay = [jnp.concatenate([jnp.broadcast_to(x[:, h:h + 1], (DK_GDN, DV_GDN)) for h in heads], axis=0)
             for x in decay]
    gate = [_stack_heads(z_ref[r, t, :]) for r, t in items]
    gate = [x * jax.nn.sigmoid(x) for x in gate]

    s_cur = [s_ref[r].reshape(H_GDN * DK_GDN, DV_GDN) for r in range(rb)]
    for c in range(nc):
        idx = [r * nc + c for r in range(rb)]
        r1 = [[_mm(wq[k][h], s_cur[r][h * DK_GDN:(h + 1) * DK_GDN].astype(bf16)) for h in heads]
              for r, k in enumerate(idx)]
        vn = [(uw[k][:, 0:DV_GDN] - jnp.concatenate([r1[r][h][0:CHUNK] for h in heads], axis=0)).astype(bf16)
              for r, k in enumerate(idx)]
        s_cur = [decay[k] * s_cur[r] + _mm(kdl[k], vn[r]) for r, k in enumerate(idx)]
        o = [jnp.concatenate([r1[r][h][CHUNK:] for h in heads], axis=0) + _mm(qk[k], vn[r])
             for r, k in enumerate(idx)]
        for r, k in enumerate(idx):
            out = (_rms(o[r], gn_ref[...]) * gate[k]).astype(bf16)
            for h in heads:
                oa_ref[r, items[k][1], h * 128:(h + 1) * 128] = out[hr[h]]
    for r in range(rb):
        s_ref[r] = s_cur[r].reshape(H_GDN, DK_GDN, DV_GDN)


def _gdn(cqkv, z, ab, s0, gn):
    b, l, _ = cqkv.shape
    tg = min(l, 256)
    rb = 4 if b % 4 == 0 else 1
    tok = lambda w: pl.BlockSpec((rb, tg, w), lambda bi, i: (bi, i, 0))
    sspec = pl.BlockSpec((rb, H_GDN, DK_GDN, DV_GDN), lambda bi, i: (bi, 0, 0, 0))
    return pl.pallas_call(
        functools.partial(_gdn_kernel, nc=tg // CHUNK, rb=rb),
        grid=(b // rb, l // tg),
        in_specs=[tok(CONV_CH), tok(128), tok(W_GDN), sspec,
                  pl.BlockSpec((1, DV_GDN), lambda bi, i: (0, 0))],
        out_specs=[tok(W_GDN), sspec],
        out_shape=[jax.ShapeDtypeStruct((b, l, W_GDN), bf16),
                   jax.ShapeDtypeStruct((b, H_GDN, DK_GDN, DV_GDN), f32)],
        compiler_params=_cparams(("arbitrary", "arbitrary")),
        name="gdn",
    )(cqkv, ab, z, s0, gn)


def _attn_kernel(*refs, tq, tk, past, seq, rb, lam_init):
    if past:
        lam_ref, gain_ref, q_ref, k_ref, v_ref, pk_ref, pv_ref, o_ref, vt_ref, acc_ref, m_ref = refs
    else:
        lam_ref, gain_ref, q_ref, k_ref, v_ref, o_ref, vt_ref, acc_ref, m_ref = refs
    i = pl.program_id(1)
    npast = past // tk
    tqe = max(tq, 128)
    heads = range(rb * H_DIFF)
    row_of = [u // H_DIFF for u in heads]
    head_of = [u % H_DIFF for u in heads]
    hs = [slice(head_of[u] * HD_DIFF, (head_of[u] + 1) * HD_DIFF) for u in heads]

    def pad_rows(x):
        if x.shape[0] == tk:
            return x
        return jnp.concatenate([x, jnp.zeros((tk - x.shape[0], x.shape[1]), x.dtype)], axis=0)

    def history(ref, t, u):
        return ref[0, row_of[u], pl.ds(t * tk * H_DIFF + head_of[u], tk, stride=H_DIFF), :]

    @pl.when(i == 0)
    def _build():
        ones = jnp.ones((V_ROWS - HD_DIFF, tk), bf16)

        def put(t, u, vtile):
            vt_ref[t, u, 0:HD_DIFF, :] = vtile.astype(f32).T.astype(bf16)
            vt_ref[t, u, HD_DIFF:V_ROWS, :] = ones

        for t in range(npast):
            for u in heads:
                put(t, u, history(pv_ref, t, u))
        for t in range(max(seq // tk, 1)):
            for u in heads:
                put(npast + t, u, pad_rows(v_ref[row_of[u], t * tk:min((t + 1) * tk, seq), hs[u]]))

    def padded_q(u):
        qu = q_ref[row_of[u], :, hs[u]]
        if tqe > tq:
            qu = jnp.concatenate([qu, jnp.zeros((tqe - tq, HD_DIFF), bf16)], axis=0)
        return qu

    lane = lax.broadcasted_iota(jnp.int32, (tqe, HD_DIFF), 1)
    zero = jnp.zeros((tqe, HD_DIFF), bf16)
    qps = [jnp.concatenate([jnp.where(lane < D_DIFF, padded_q(u), zero),
                            jnp.where(lane >= D_DIFF, padded_q(u), zero)], axis=0) for u in heads]

    acc_ref[...] = jnp.zeros_like(acc_ref)
    m_ref[...] = jnp.full(m_ref.shape, NEG, f32)

    def scores(kts):
        return [_mm_nt(kts[h], qps[h]) for h in heads]

    def process(tiles):
        sts = []
        for _, st, masked in tiles:
            if masked:
                krow = lax.broadcasted_iota(jnp.int32, (tk, 2 * tqe), 0)
                qcol = lax.broadcasted_iota(jnp.int32, (tk, 2 * tqe), 1)
                qcol = jnp.where(qcol >= tqe, qcol - tqe, qcol)
                visible = (krow >> 6) <= (qcol >> 6)
                st = [jnp.where(visible, x, NEG) for x in st]
            sts.append(st)
        m_olds = [m_ref[h] for h in heads]
        m_news = m_olds
        for st in sts:
            m_news = [jnp.maximum(m_news[h], jnp.max(st[h], axis=0, keepdims=True)) for h in heads]
        alphas = [jnp.exp2(m_olds[h] - m_news[h]) for h in heads]
        pvs = None
        for (j, _, _), st in zip(tiles, sts):
            ps = [jnp.exp2(st[h] - m_news[h]).astype(bf16) for h in heads]
            pv = [_mm(vt_ref[j, h], ps[h]) for h in heads]
            pvs = pv if pvs is None else [pvs[h] + pv[h] for h in heads]
        for h in heads:
            acc_ref[h] = alphas[h] * acc_ref[h] + pvs[h]
            m_ref[h] = m_news[h]

    if past:
        diag_keys = [pad_rows(k_ref[row_of[u], :, hs[u]]) for u in heads]
        keys = [[history(pk_ref, t, u).astype(bf16) for u in heads] for t in range(npast)] + [diag_keys]
        for j in range(0, npast + 1, 2):
            process([(t, scores(keys[t]), t == npast) for t in range(j, min(j + 2, npast + 1))])
    else:
        def keys(j):
            rows = pl.ds(pl.multiple_of(j * tk, tk), tk)
            return [k_ref[row_of[u], rows, hs[u]] for u in heads]

        def body(jj, carry):
            j = 2 * jj
            process([(j, scores(keys(j)), False), (j + 1, scores(keys(j + 1)), False)])
            return carry

        lax.fori_loop(0, i // 2, body, 0)
        odd = lax.rem(i, 2) == 1

        @pl.when(odd)
        def _():
            process([(i - 1, scores(keys(i - 1)), False), (i, scores(keys(i)), True)])

        @pl.when(jnp.logical_not(odd))
        def _():
            process([(i, scores(keys(i)), True)])

    lam = (jnp.exp(jnp.sum(lam_ref[0:1, :] * lam_ref[1:2, :], axis=-1, keepdims=True))
           - jnp.exp(jnp.sum(lam_ref[2:3, :] * lam_ref[3:4, :], axis=-1, keepdims=True)) + lam_init)
    for h in heads:
        acc = acc_ref[h]
        o0 = acc[0:HD_DIFF, 0:tqe] / acc[HD_DIFF:HD_DIFF + 1, 0:tqe]
        o1 = acc[0:HD_DIFF, tqe:] / acc[HD_DIFF:HD_DIFF + 1, tqe:]
        ot = o0 - lam * o1
        ms = jnp.mean(ot * ot, axis=0, keepdims=True)
        o = (ot * lax.rsqrt(ms + EPS)).T * gain_ref[...] * (1.0 - lam_init)
        o_ref[row_of[h], :, hs[h]] = o[0:tq].astype(bf16)


def _attn(lamp, gain, qs, kb, vb, hist_k, hist_v, layer, lam_init):
    b, l, _ = qs.shape
    tk = ATT_TILE
    tq = min(l, tk)
    past = 0 if hist_k is None else hist_k.shape[2] // H_DIFF
    rb = 2 if (b % 2 == 0 and past == 0) else 1
    assert past % tk == 0 and (l % tk == 0 or l < tk)
    assert past == 0 or l <= tk
    nt = past // tk + max(l // tk, 1)
    tqe = max(tq, 128)
    qspec = pl.BlockSpec((rb, tq, W_DIFF), lambda bi, i: (bi, i, 0))
    kspec = pl.BlockSpec((rb, l, W_DIFF), lambda bi, i: (bi, 0, 0), pipeline_mode=pl.Buffered(1))
    in_specs = [pl.BlockSpec((8, 128), lambda bi, i: (0, 0)),
                pl.BlockSpec((1, HD_DIFF), lambda bi, i: (0, 0)),
                qspec, kspec, kspec]
    args = [lamp, gain, qs, kb, vb]
    scratch = [pltpu.VMEM((nt, rb * H_DIFF, V_ROWS, tk), bf16),
               pltpu.VMEM((rb * H_DIFF, V_ROWS, 2 * tqe), f32),
               pltpu.VMEM((rb * H_DIFF, 1, 2 * tqe), f32)]
    if past:
        pspec = pl.BlockSpec((1, rb, past * H_DIFF, HD_DIFF), lambda bi, i: (layer, bi, 0, 0),
                             pipeline_mode=pl.Buffered(1))
        in_specs += [pspec, pspec]
        args += [hist_k, hist_v]
    return pl.pallas_call(
        functools.partial(_attn_kernel, tq=tq, tk=tk, past=past, seq=l, rb=rb, lam_init=lam_init),
        grid=(b // rb, l // tq),
        in_specs=in_specs,
        out_specs=qspec,
        out_shape=jax.ShapeDtypeStruct((b, l, W_DIFF), bf16),
        scratch_shapes=scratch,
        compiler_params=_cparams(("arbitrary", "arbitrary")),
        name="attn",
    )(*args)


def _mix_kernel(x_ref, oa_ref, ob_ref, wo_ref, ln_ref, wq_ref, mk_ref, mv_ref, wmo_ref, h_ref):
    h1 = x_ref[0] + _mm(oa_ref[0], wo_ref[0:W_GDN, :]) + _mm(ob_ref[0], wo_ref[W_GDN:, :])
    hn = _rms(h1, ln_ref[...]).astype(bf16)
    q = (_mm(hn, wq_ref[...]) * (D_MEM ** -0.5 * LOG2E)).astype(bf16)
    parts = []
    for h in range(H_MEM):
        cs = slice(h * D_MEM, (h + 1) * D_MEM)
        s = _mm_nt(q[:, cs], mk_ref[0, :, cs])
        p = jnp.exp2(s - jnp.max(s, axis=-1, keepdims=True))
        den = jnp.sum(p, axis=-1, keepdims=True)
        parts.append(_mm(p.astype(bf16), mv_ref[0, :, cs]) / den)
    o = jnp.concatenate(parts, axis=-1).astype(bf16)
    h_ref[0] = h1 + _mm(o, wmo_ref[...])


def _mix(x, oa, ob, w_out, ln, wq, mk, mv, wmo):
    b, l, _ = x.shape
    tm = min(l, 1024)
    n_mem = mk.shape[1]
    tok = lambda w: pl.BlockSpec((1, tm, w), lambda bi, i: (bi, i, 0))
    full = lambda s: pl.BlockSpec(s, lambda bi, i: (0,) * len(s), pipeline_mode=pl.Buffered(1))
    mspec = pl.BlockSpec((1, n_mem, D_MODEL), lambda bi, i: (bi, 0, 0))
    return pl.pallas_call(
        _mix_kernel,
        grid=(b, l // tm),
        in_specs=[tok(D_MODEL), tok(W_GDN), tok(W_DIFF), full((D_MODEL, D_MODEL)), full((1, D_MODEL)),
                  full((D_MODEL, D_MODEL)), mspec, mspec, full((D_MODEL, D_MODEL))],
        out_specs=tok(D_MODEL),
        out_shape=jax.ShapeDtypeStruct((b, l, D_MODEL), f32),
        compiler_params=_cparams(("arbitrary", "arbitrary")),
        name="mix",
    )(x, oa, ob, w_out, ln, wq, mk, mv, wmo)


def _ffn_kernel(h_ref, ln_ref, w1_ref, w2_ref, lnf_ref, o_ref, *, final):
    h = h_ref[...]
    hn = _rms(h, ln_ref[...]).astype(bf16)
    acc = h
    step = 1024
    for c in range(D_FF // step):
        a = jnp.maximum(_mm(hn, w1_ref[:, c * step:(c + 1) * step]), 0.0)
        acc = acc + _mm((a * a).astype(bf16), w2_ref[c * step:(c + 1) * step, :])
    if final:
        acc = _rms(acc, lnf_ref[...])
    o_ref[...] = acc


def _ffn(h2d, ln, w1, w2, lnf, final):
    t = h2d.shape[0]
    tm = min(t, 512)
    full = lambda s: pl.BlockSpec(s, lambda i: (0,) * len(s), pipeline_mode=pl.Buffered(1))
    tok = pl.BlockSpec((tm, D_MODEL), lambda i: (i, 0))
    return pl.pallas_call(
        functools.partial(_ffn_kernel, final=final),
        grid=(t // tm,),
        in_specs=[tok, full((1, D_MODEL)), full((D_MODEL, D_FF)), full((D_FF, D_MODEL)), full((1, D_MODEL))],
        out_specs=tok,
        out_shape=jax.ShapeDtypeStruct((t, D_MODEL), f32),
        compiler_params=_cparams(("arbitrary",)),
        name="ffn",
    )(h2d, ln, w1, w2, lnf)


def _pad_lanes(x, width=128):
    return jnp.pad(x, ((0, 0), (0, width - x.shape[-1])))


def kernel(x_prompt, x_sample, mem_prompt, cache_diff_k, cache_diff_v, cache_mem_k, cache_mem_v,
           state_gdn, state_gdn_conv, ln_mix, w_in, conv_w, a_log, dt_bias, gdn_norm,
           lambda_q1, lambda_k1, lambda_q2, lambda_k2, diff_norm, w_out, ln_mem_q, ln_mem_kv,
           w_mem_q, w_mem_k, w_mem_v, w_mem_o, ln_ffn, w_ff1, w_ff2, ln_final):
    depth = w_in.shape[0]
    bp, lp, _ = x_prompt.shape
    bs, ls, _ = x_sample.shape
    n_mem = mem_prompt.shape[1]
    past = cache_diff_k.shape[2]

    w_main = jnp.concatenate([w_in[:, :, :AB_OFF], w_in[:, :, AB_OFF + 2 * H_GDN:]], axis=2).astype(bf16)
    w_ab = jnp.pad(w_in[:, :, AB_OFF:AB_OFF + 2 * H_GDN], ((0, 0), (0, 0), (0, 128 - 2 * H_GDN))).astype(bf16)
    w_out_b, wq_b, wk_b, wv_b, wmo_b = (w.astype(bf16) for w in (w_out, w_mem_q, w_mem_k, w_mem_v, w_mem_o))
    w1_b, w2_b = w_ff1.astype(bf16), w_ff2.astype(bf16)
    alog_p = _pad_lanes(a_log)
    dtb_p = _pad_lanes(dt_bias)
    lamp = jnp.stack([_pad_lanes(p) for p in (lambda_q1, lambda_k1, lambda_q2, lambda_k2)], axis=1)
    lamp = jnp.pad(lamp, ((0, 0), (0, 4), (0, 0)))
    lnf = ln_final.reshape(1, D_MODEL)

    mk_all, mv_all, mkb_all, mvb_all = _memkv(mem_prompt.reshape(bp * n_mem, D_MODEL),
                                              ln_mem_kv.reshape(depth, 1, D_MODEL), wk_b, wv_b)

    cbuf_p = jnp.zeros((bp, 8, CONV_CH), f32)
    s0_p = jnp.zeros((bp, H_GDN, DK_GDN, DV_GDN), f32)
    cbuf_s_all = jnp.pad(state_gdn_conv, ((0, 0), (0, 0), (8 - (CONV_W - 1), 0), (0, 0)))
    cmk_b = cache_mem_k.reshape(depth, bs, n_mem, D_MODEL).astype(bf16)
    cmv_b = cache_mem_v.reshape(depth, bs, n_mem, D_MODEL).astype(bf16)
    hist_k = cache_diff_k.reshape(depth, bs, past * H_DIFF, HD_DIFF)
    hist_v = cache_diff_v.reshape(depth, bs, past * H_DIFF, HD_DIFF)

    def layer(l, x, cbuf8, s0, hk, hv, mk, mv, kd_all, vd_all, final):
        b, ln_, _ = x.shape
        lam_init = 0.8 - 0.6 * math.exp(-0.3 * l)
        cqkv, z, ab, qs, kd_all, vd_all, kb, vb, cnew = _inproj(
            x, ln_mix[l].reshape(1, D_MODEL), w_main[l], w_ab[l], conv_w[l],
            alog_p[l:l + 1], dtb_p[l:l + 1], cbuf8, l, depth, kd_all, vd_all)
        oa, s_new = _gdn(cqkv, z, ab, s0, gdn_norm[l].reshape(1, DV_GDN))
        ob = _attn(lamp[l], diff_norm[l].reshape(1, HD_DIFF), qs, kb, vb, hk, hv, l, lam_init)
        h = _mix(x, oa, ob, w_out_b[l], ln_mem_q[l].reshape(1, D_MODEL), wq_b[l], mk, mv, wmo_b[l])
        h = _ffn(h.reshape(b * ln_, D_MODEL), ln_ffn[l].reshape(1, D_MODEL), w1_b[l], w2_b[l], lnf, final)
        return h.reshape(b, ln_, D_MODEL), kd_all, vd_all, s_new, cnew[:, 8 - (CONV_W - 1):, :]

    hp, hs = x_prompt, x_sample
    pk_ = pv_ = sk_ = sv_ = None
    ps_, pc_, ss_, sc_ = [], [], [], []
    for l in range(depth):
        final = l == depth - 1
        hp, pk_, pv_, s_, c_ = layer(l, hp, cbuf_p, s0_p, None, None,
                                     mkb_all[l].reshape(bp, n_mem, D_MODEL),
                                     mvb_all[l].reshape(bp, n_mem, D_MODEL), pk_, pv_, final)
        ps_.append(s_); pc_.append(c_)
        hs, sk_, sv_, s_, c_ = layer(l, hs, cbuf_s_all[l], state_gdn[l], hist_k, hist_v,
                                     cmk_b[l], cmv_b[l], sk_, sv_, final)
        ss_.append(s_); sc_.append(c_)

    mem_shape = (depth, bp, n_mem, H_MEM, D_MEM)
    return (hp, hs,
            pk_.reshape(depth, bp, lp, H_DIFF, HD_DIFF), pv_.reshape(depth, bp, lp, H_DIFF, HD_DIFF),
            jnp.stack(ps_), jnp.stack(pc_),
            mk_all.reshape(mem_shape), mv_all.reshape(mem_shape),
            sk_.reshape(depth, bs, ls, H_DIFF, HD_DIFF), sv_.reshape(depth, bs, ls, H_DIFF, HD_DIFF),
            jnp.stack(ss_), jnp.stack(sc_))
```

```python
import functools
import math

import jax
import jax.numpy as jnp
from jax import lax
from jax.experimental import pallas as pl
from jax.experimental.pallas import tpu as pltpu

D_MODEL = 1024
CHUNK = 64
H_GDN = 4
DK_GDN = 128
DV_GDN = 128
CONV_W = 4
W_GDN = H_GDN * DK_GDN
CONV_CH = 3 * W_GDN
H_DIFF = 4
D_DIFF = 64
HD_DIFF = 2 * D_DIFF
W_DIFF = H_DIFF * HD_DIFF
H_MEM = 4
D_MEM = 256
D_FF = 4 * D_MODEL
EPS = 1e-6
LOG2E = 1.4426950408889634
NEG = -1e30
MAIN_COLS = CONV_CH + 4 * 512
AB_OFF = CONV_CH + W_GDN
ROWS_GDN = H_GDN * CHUNK
V_ROWS = HD_DIFF + 16
ATT_TILE = 256
ROWS_MEMKV = 512
ROWS_INPROJ = 512
ROWS_GDN = 256
ROWS_MIX = 1024
ROWS_FFN = 1024
BATCH_ROWS_GDN = 4
BATCH_ROWS_ATT = 2
VMEM_LIMIT = 56 * 1024 * 1024

f32 = jnp.float32
bf16 = jnp.bfloat16


def _cparams(sem):
    return pltpu.CompilerParams(dimension_semantics=sem, vmem_limit_bytes=VMEM_LIMIT)


def _rms(x, g):
    ms = jnp.mean(x * x, axis=-1, keepdims=True)
    return x * lax.rsqrt(ms + EPS) * g


def _mm(a, b):
    return jnp.dot(a, b, preferred_element_type=f32)


def _mm_nt(a, b):
    return lax.dot_general(a, b, (((1,), (1,)), ((), ())), preferred_element_type=f32)


def _memkv_kernel(mem_ref, g_ref, wk_ref, wv_ref, mk_ref, mv_ref, mkb_ref, mvb_ref):
    xn = _rms(mem_ref[...], g_ref[0]).astype(bf16)
    k = _mm(xn, wk_ref[0])
    v = _mm(xn, wv_ref[0])
    mk_ref[0] = k
    mv_ref[0] = v
    mkb_ref[0] = k.astype(bf16)
    mvb_ref[0] = v.astype(bf16)


def _memkv(mem2d, ln, wk, wv):
    depth = wk.shape[0]
    t = mem2d.shape[0]
    tm = min(t, ROWS_MEMKV)
    out = jax.ShapeDtypeStruct((depth, t, D_MODEL), f32)
    outb = jax.ShapeDtypeStruct((depth, t, D_MODEL), bf16)
    wspec = pl.BlockSpec((1, D_MODEL, D_MODEL), lambda l, i: (l, 0, 0))
    ospec = pl.BlockSpec((1, tm, D_MODEL), lambda l, i: (l, i, 0))
    return pl.pallas_call(
        _memkv_kernel,
        grid=(depth, t // tm),
        in_specs=[pl.BlockSpec((tm, D_MODEL), lambda l, i: (i, 0)),
                  pl.BlockSpec((1, 1, D_MODEL), lambda l, i: (l, 0, 0)),
                  wspec, wspec],
        out_specs=[ospec, ospec, ospec, ospec],
        out_shape=[out, out, outb, outb],
        compiler_params=_cparams(("arbitrary", "arbitrary")),
        name="memkv",
    )(mem2d, ln, wk, wv)


def _inproj_kernel(*refs, tm, aliased):
    if aliased:
        refs = refs[:8] + refs[10:]
    (x_ref, ln_ref, w_ref, wab_ref, cw_ref, alog_ref, dtb_ref, cbuf_ref,
     cqkv_ref, z_ref, ab_ref, qs_ref, kd_ref, vd_ref, kb_ref, vb_ref, cnew_ref, cs_ref) = refs
    i = pl.program_id(1)

    @pl.when(i == 0)
    def _():
        cs_ref[0:8, :] = cbuf_ref[0]

    xn = _rms(x_ref[0], ln_ref[...]).astype(bf16)
    conv_in = _mm(xn, w_ref[:, 0:CONV_CH])
    cs_ref[8:8 + tm, :] = conv_in
    for blk in range(CONV_CH // 128):
        cols = slice(blk * 128, (blk + 1) * 128)
        y = cw_ref[0:1, cols] * cs_ref[5:5 + tm, cols]
        y = y + cw_ref[1:2, cols] * cs_ref[6:6 + tm, cols]
        y = y + cw_ref[2:3, cols] * cs_ref[7:7 + tm, cols]
        y = y + cw_ref[3:4, cols] * cs_ref[8:8 + tm, cols]
        c = y * jax.nn.sigmoid(y)
        if blk < 2 * H_GDN:
            c = c * lax.rsqrt(jnp.sum(c * c, axis=-1, keepdims=True) + EPS)
            if blk < H_GDN:
                c = c * DK_GDN ** -0.5
        cqkv_ref[0, :, cols] = c
    tail = cs_ref[tm:tm + 8, :]
    cnew_ref[0] = tail
    cs_ref[0:8, :] = tail

    z_ref[0] = _mm(xn, w_ref[:, CONV_CH:CONV_CH + 512])
    qd = _mm(xn, w_ref[:, CONV_CH + 512:CONV_CH + 1024])
    qs_ref[0] = (qd * (D_DIFF ** -0.5 * LOG2E)).astype(bf16)
    kd = _mm(xn, w_ref[:, CONV_CH + 1024:CONV_CH + 1536])
    kb_ref[0] = kd.astype(bf16)
    vd = _mm(xn, w_ref[:, CONV_CH + 1536:CONV_CH + 2048])
    vb_ref[0] = vd.astype(bf16)
    for h in range(H_DIFF):
        kd_ref[0, 0, pl.ds(h, tm, stride=H_DIFF), :] = kd[:, h * HD_DIFF:(h + 1) * HD_DIFF]
        vd_ref[0, 0, pl.ds(h, tm, stride=H_DIFF), :] = vd[:, h * HD_DIFF:(h + 1) * HD_DIFF]

    ab = _mm(xn, wab_ref[...])
    sp_in = ab + dtb_ref[...]
    softplus = jnp.maximum(sp_in, 0.0) + jnp.log1p(jnp.exp(-jnp.abs(sp_in)))
    g = -jnp.exp(alog_ref[...]) * softplus
    beta = jax.nn.sigmoid(ab)
    lane = lax.broadcasted_iota(jnp.int32, ab.shape, 1)
    ab_ref[0] = jnp.where(lane < H_GDN, g, beta)


def _inproj(x, ln, w_main, w_ab, conv_w, alog, dtb, cbuf8, layer, depth, kd_all, vd_all):
    b, l, _ = x.shape
    tm = min(l, ROWS_INPROJ)
    nt = l // tm
    aliased = kd_all is not None
    tok = lambda w: pl.BlockSpec((1, tm, w), lambda bi, i: (bi, i, 0))
    full = lambda s: pl.BlockSpec(s, lambda bi, i: (0,) * len(s), pipeline_mode=pl.Buffered(1))
    cache = jax.ShapeDtypeStruct((depth, b, l * H_DIFF, HD_DIFF), f32)
    cache_spec = pl.BlockSpec((1, 1, tm * H_DIFF, HD_DIFF), lambda bi, i: (layer, bi, i, 0))
    out_shape = [
        jax.ShapeDtypeStruct((b, l, CONV_CH), f32),
        jax.ShapeDtypeStruct((b, l, W_GDN), f32),
        jax.ShapeDtypeStruct((b, l, 128), f32),
        jax.ShapeDtypeStruct((b, l, W_DIFF), bf16),
        cache, cache,
        jax.ShapeDtypeStruct((b, l, W_DIFF), bf16),
        jax.ShapeDtypeStruct((b, l, W_DIFF), bf16),
        jax.ShapeDtypeStruct((b, 8, CONV_CH), f32),
    ]
    out_specs = [tok(CONV_CH), tok(W_GDN), tok(128), tok(W_DIFF), cache_spec, cache_spec,
                 tok(W_DIFF), tok(W_DIFF),
                 pl.BlockSpec((1, 8, CONV_CH), lambda bi, i: (bi, 0, 0))]
    in_specs = [tok(D_MODEL), full((1, D_MODEL)), full((D_MODEL, MAIN_COLS)), full((D_MODEL, 128)),
                full((CONV_W, CONV_CH)), full((1, 128)), full((1, 128)),
                pl.BlockSpec((1, 8, CONV_CH), lambda bi, i: (bi, 0, 0))]
    args = [x, ln, w_main, w_ab, conv_w, alog, dtb, cbuf8]
    aliases = {}
    if aliased:
        in_specs += [pl.BlockSpec(memory_space=pl.ANY), pl.BlockSpec(memory_space=pl.ANY)]
        args += [kd_all, vd_all]
        aliases = {8: 4, 9: 5}
    return pl.pallas_call(
        functools.partial(_inproj_kernel, tm=tm, aliased=aliased),
        grid=(b, nt),
        in_specs=in_specs,
        out_specs=out_specs,
        out_shape=out_shape,
        input_output_aliases=aliases,
        scratch_shapes=[pltpu.VMEM((tm + 8, CONV_CH), f32)],
        compiler_params=_cparams(("arbitrary", "arbitrary")),
        name="inproj",
    )(*args)


def _split3(x):
    hi = x.astype(bf16)
    r1 = x - hi.astype(f32)
    mid = r1.astype(bf16)
    lo = (r1 - mid.astype(f32)).astype(bf16)
    return hi, mid, lo


def _mmb(a, b):
    return _mm(a.astype(bf16), b.astype(bf16))


def _unit_lower_inverses(mats, row, col):
    n_rows = row.shape[0]

    def blk(n):
        sh = n.bit_length() - 1
        return (row >> sh) == (col >> sh)

    eye = (row == col).astype(f32)
    a8 = [jnp.where(blk(8), a, 0.0) for a in mats]
    p = [_mmb(x, x) for x in a8]
    q = [_mmb(x, x) for x in p]
    t = [_mmb(eye - x, eye + y) for x, y in zip(a8, p)]
    t = [_mmb(x, eye + y) for x, y in zip(t, q)]
    for n in (8, 16, 32):
        sel = jnp.logical_and(blk(2 * n), jnp.logical_not(blk(n)))
        off = [jnp.where(sel, a, 0.0).astype(bf16) for a in mats]
        starts = range(0, n_rows, 2 * n)
        low = [jnp.concatenate([ti[r + n:r + 2 * n] for r in starts], axis=0) for ti in t]
        x = [_mm(li.astype(bf16), oi) for li, oi in zip(low, off)]
        low = [li - _mmb(xi, ti) for li, xi, ti in zip(low, x, t)]
        t = [jnp.concatenate([piece for k, r in enumerate(starts)
                              for piece in (ti[r:r + n], li[k * n:(k + 1) * n])], axis=0)
             for ti, li in zip(t, low)]
    return t


def _stack_heads(x):
    return jnp.concatenate([x[:, h * 128:(h + 1) * 128] for h in range(H_GDN)], axis=0)


def _gdn_kernel(cqkv_ref, ab_ref, z_ref, s0_ref, gn_ref, oa_ref, s_ref, *, nc, rb):
    i = pl.program_id(1)

    @pl.when(i == 0)
    def _():
        s_ref[...] = s0_ref[...]

    n = ROWS_GDN
    row = lax.broadcasted_iota(jnp.int32, (n, n), 0)
    col = lax.broadcasted_iota(jnp.int32, (n, n), 1)
    same = (row >> 6) == (col >> 6)
    lower = jnp.logical_and(same, row >= col)
    strict = jnp.logical_and(same, row > col)
    r64 = lax.broadcasted_iota(jnp.int32, (CHUNK, CHUNK), 0)
    c64 = lax.broadcasted_iota(jnp.int32, (CHUNK, CHUNK), 1)
    tril64 = (r64 >= c64).astype(bf16)
    head_of_col = lax.broadcasted_iota(jnp.int32, (DK_GDN, n), 1) >> 6
    heads = range(H_GDN)
    hr = [slice(h * CHUNK, (h + 1) * CHUNK) for h in heads]

    def bcast_col(m, lane):
        return jnp.concatenate(
            [jnp.broadcast_to(m[:, lane + h:lane + h + 1], (m.shape[0], 128)) for h in heads], axis=0)

    items = [(r, slice(c * CHUNK, (c + 1) * CHUNK)) for r in range(rb) for c in range(nc)]

    gb = [ab_ref[r, t, :] for r, t in items]
    parts = [_split3(x) for x in gb]
    gcum = [_mm(tril64, hi) + _mm(tril64, mid) + _mm(tril64, lo) for hi, mid, lo in parts]
    gs = [bcast_col(x, 0) for x in gcum]
    bs = [bcast_col(x, H_GDN) for x in gb]
    glast = [x[CHUNK - 1:CHUNK, :] for x in gcum]
    gl = [bcast_col(jnp.broadcast_to(x, (CHUNK, 128)), 0) for x in glast]
    grow = [x.T[0:1, :] for x in gs]
    gam = [jnp.where(lower, jnp.exp(jnp.where(lower, jnp.concatenate([x, x], axis=1) - y, 0.0)), 0.0)
           for x, y in zip(gs, grow)]
    ks = [_stack_heads(cqkv_ref[r, t, W_GDN:2 * W_GDN]) for r, t in items]
    qs = [_stack_heads(cqkv_ref[r, t, 0:W_GDN]) for r, t in items]
    vs = [_stack_heads(cqkv_ref[r, t, 2 * W_GDN:CONV_CH]) for r, t in items]
    kq = [_mm_nt(jnp.concatenate([k, q], axis=0).astype(bf16), k.astype(bf16)) for k, q in zip(ks, qs)]
    a = [jnp.where(strict, jnp.concatenate([b_, b_], axis=1) * x[0:n] * g, 0.0) for b_, x, g in zip(bs, kq, gam)]
    qk = [(x[n:2 * n] * g).astype(bf16) for x, g in zip(kq, gam)]
    t_inv = _unit_lower_inverses(a, row, col)
    eg = [jnp.exp(x) for x in gs]
    uw = [_mmb(ti, jnp.concatenate([b_ * v, b_ * e * k], axis=1))
          for ti, b_, v, e, k in zip(t_inv, bs, vs, eg, ks)]
    wq = [[jnp.concatenate([x[hr[h], DV_GDN:], (q * e)[hr[h]]], axis=0).astype(bf16) for h in heads]
          for x, q, e in zip(uw, qs, eg)]
    kdec_t = [(k * jnp.exp(l_ - g)).T for k, l_, g in zip(ks, gl, gs)]
    kdl = [jnp.concatenate([jnp.where(head_of_col == h, x, 0.0) for h in heads], axis=0).astype(bf16)
           for x in kdec_t]
    decay = [jnp.exp(x) for x in glast]
    decay = [jnp.concatenate([jnp.broadcast_to(x[:, h:h + 1], (DK_GDN, DV_GDN)) for h in heads], axis=0)
             for x in decay]
    gate = [_stack_heads(z_ref[r, t, :]) for r, t in items]
    gate = [x * jax.nn.sigmoid(x) for x in gate]

    s_cur = [s_ref[r].reshape(H_GDN * DK_GDN, DV_GDN) for r in range(rb)]
    for c in range(nc):
        idx = [r * nc + c for r in range(rb)]
        r1 = [[_mm(wq[k][h], s_cur[r][h * DK_GDN:(h + 1) * DK_GDN].astype(bf16)) for h in heads]
              for r, k in enumerate(idx)]
        vn = [(uw[k][:, 0:DV_GDN] - jnp.concatenate([r1[r][h][0:CHUNK] for h in heads], axis=0)).astype(bf16)
              for r, k in enumerate(idx)]
        s_cur = [decay[k] * s_cur[r] + _mm(kdl[k], vn[r]) for r, k in enumerate(idx)]
        o = [jnp.concatenate([r1[r][h][CHUNK:] for h in heads], axis=0) + _mm(qk[k], vn[r])
             for r, k in enumerate(idx)]
        for r, k in enumerate(idx):
            out = (_rms(o[r], gn_ref[...]) * gate[k]).astype(bf16)
            for h in heads:
                oa_ref[r, items[k][1], h * 128:(h + 1) * 128] = out[hr[h]]
    for r in range(rb):
        s_ref[r] = s_cur[r].reshape(H_GDN, DK_GDN, DV_GDN)


def _gdn(cqkv, z, ab, s0, gn):
    b, l, _ = cqkv.shape
    tg = min(l, ROWS_GDN)
    rb = BATCH_ROWS_GDN if b % BATCH_ROWS_GDN == 0 else 1
    tok = lambda w: pl.BlockSpec((rb, tg, w), lambda bi, i: (bi, i, 0))
    sspec = pl.BlockSpec((rb, H_GDN, DK_GDN, DV_GDN), lambda bi, i: (bi, 0, 0, 0))
    return pl.pallas_call(
        functools.partial(_gdn_kernel, nc=tg // CHUNK, rb=rb),
        grid=(b // rb, l // tg),
        in_specs=[tok(CONV_CH), tok(128), tok(W_GDN), sspec,
                  pl.BlockSpec((1, DV_GDN), lambda bi, i: (0, 0))],
        out_specs=[tok(W_GDN), sspec],
        out_shape=[jax.ShapeDtypeStruct((b, l, W_GDN), bf16),
                   jax.ShapeDtypeStruct((b, H_GDN, DK_GDN, DV_GDN), f32)],
        compiler_params=_cparams(("arbitrary", "arbitrary")),
        name="gdn",
    )(cqkv, ab, z, s0, gn)


def _attn_kernel(*refs, tq, tk, past, seq, rb, lam_init):
    if past:
        lam_ref, gain_ref, q_ref, k_ref, v_ref, pk_ref, pv_ref, o_ref, vt_ref, acc_ref, m_ref = refs
    else:
        lam_ref, gain_ref, q_ref, k_ref, v_ref, o_ref, vt_ref, acc_ref, m_ref = refs
    i = pl.program_id(1)
    npast = past // tk
    tqe = max(tq, 128)
    heads = range(rb * H_DIFF)
    row_of = [u // H_DIFF for u in heads]
    head_of = [u % H_DIFF for u in heads]
    hs = [slice(head_of[u] * HD_DIFF, (head_of[u] + 1) * HD_DIFF) for u in heads]

    def pad_rows(x):
        if x.shape[0] == tk:
            return x
        return jnp.concatenate([x, jnp.zeros((tk - x.shape[0], x.shape[1]), x.dtype)], axis=0)

    def history(ref, t, u):
        return ref[0, row_of[u], pl.ds(t * tk * H_DIFF + head_of[u], tk, stride=H_DIFF), :]

    @pl.when(i == 0)
    def _build():
        ones = jnp.ones((V_ROWS - HD_DIFF, tk), bf16)

        def put(t, u, vtile):
            vt_ref[t, u, 0:HD_DIFF, :] = vtile.astype(f32).T.astype(bf16)
            vt_ref[t, u, HD_DIFF:V_ROWS, :] = ones

        for t in range(npast):
            for u in heads:
                put(t, u, history(pv_ref, t, u))
        for t in range(max(seq // tk, 1)):
            for u in heads:
                put(npast + t, u, pad_rows(v_ref[row_of[u], t * tk:min((t + 1) * tk, seq), hs[u]]))

    def padded_q(u):
        qu = q_ref[row_of[u], :, hs[u]]
        if tqe > tq:
            qu = jnp.concatenate([qu, jnp.zeros((tqe - tq, HD_DIFF), bf16)], axis=0)
        return qu

    lane = lax.broadcasted_iota(jnp.int32, (tqe, HD_DIFF), 1)
    zero = jnp.zeros((tqe, HD_DIFF), bf16)
    qps = [jnp.concatenate([jnp.where(lane < D_DIFF, padded_q(u), zero),
                            jnp.where(lane >= D_DIFF, padded_q(u), zero)], axis=0) for u in heads]

    acc_ref[...] = jnp.zeros_like(acc_ref)
    m_ref[...] = jnp.full(m_ref.shape, NEG, f32)

    def scores(kts):
        return [_mm_nt(kts[h], qps[h]) for h in heads]

    def process(tiles):
        sts = []
        for _, st, masked in tiles:
            if masked:
                krow = lax.broadcasted_iota(jnp.int32, (tk, 2 * tqe), 0)
                qcol = lax.broadcasted_iota(jnp.int32, (tk, 2 * tqe), 1)
                qcol = jnp.where(qcol >= tqe, qcol - tqe, qcol)
                visible = (krow >> 6) <= (qcol >> 6)
                st = [jnp.where(visible, x, NEG) for x in st]
            sts.append(st)
        m_olds = [m_ref[h] for h in heads]
        m_news = m_olds
        for st in sts:
            m_news = [jnp.maximum(m_news[h], jnp.max(st[h], axis=0, keepdims=True)) for h in heads]
        alphas = [jnp.exp2(m_olds[h] - m_news[h]) for h in heads]
        pvs = None
        for (j, _, _), st in zip(tiles, sts):
            ps = [jnp.exp2(st[h] - m_news[h]).astype(bf16) for h in heads]
            pv = [_mm(vt_ref[j, h], ps[h]) for h in heads]
            pvs = pv if pvs is None else [pvs[h] + pv[h] for h in heads]
        for h in heads:
            acc_ref[h] = alphas[h] * acc_ref[h] + pvs[h]
            m_ref[h] = m_news[h]

    if past:
        diag_keys = [pad_rows(k_ref[row_of[u], :, hs[u]]) for u in heads]
        keys = [[history(pk_ref, t, u).astype(bf16) for u in heads] for t in range(npast)] + [diag_keys]
        for j in range(0, npast + 1, 2):
            process([(t, scores(keys[t]), t == npast) for t in range(j, min(j + 2, npast + 1))])
    else:
        def keys(j):
            rows = pl.ds(pl.multiple_of(j * tk, tk), tk)
            return [k_ref[row_of[u], rows, hs[u]] for u in heads]

        def body(jj, carry):
            j = 2 * jj
            process([(j, scores(keys(j)), False), (j + 1, scores(keys(j + 1)), False)])
            return carry

        lax.fori_loop(0, i // 2, body, 0)
        odd = lax.rem(i, 2) == 1

        @pl.when(odd)
        def _():
            process([(i - 1, scores(keys(i - 1)), False), (i, scores(keys(i)), True)])

        @pl.when(jnp.logical_not(odd))
        def _():
            process([(i, scores(keys(i)), True)])

    lam = (jnp.exp(jnp.sum(lam_ref[0:1, :] * lam_ref[1:2, :], axis=-1, keepdims=True))
           - jnp.exp(jnp.sum(lam_ref[2:3, :] * lam_ref[3:4, :], axis=-1, keepdims=True)) + lam_init)
    for h in heads:
        acc = acc_ref[h]
        o0 = acc[0:HD_DIFF, 0:tqe] / acc[HD_DIFF:HD_DIFF + 1, 0:tqe]
        o1 = acc[0:HD_DIFF, tqe:] / acc[HD_DIFF:HD_DIFF + 1, tqe:]
        ot = o0 - lam * o1
        ms = jnp.mean(ot * ot, axis=0, keepdims=True)
        o = (ot * lax.rsqrt(ms + EPS)).T * gain_ref[...] * (1.0 - lam_init)
        o_ref[row_of[h], :, hs[h]] = o[0:tq].astype(bf16)


def _attn(lamp, gain, qs, kb, vb, hist_k, hist_v, layer, lam_init):
    b, l, _ = qs.shape
    tk = ATT_TILE
    tq = min(l, tk)
    past = 0 if hist_k is None else hist_k.shape[2] // H_DIFF
    rb = BATCH_ROWS_ATT if (b % BATCH_ROWS_ATT == 0 and past == 0) else 1
    assert past % tk == 0 and (l % tk == 0 or l < tk)
    assert past == 0 or l <= tk
    nt = past // tk + max(l // tk, 1)
    tqe = max(tq, 128)
    qspec = pl.BlockSpec((rb, tq, W_DIFF), lambda bi, i: (bi, i, 0))
    kspec = pl.BlockSpec((rb, l, W_DIFF), lambda bi, i: (bi, 0, 0), pipeline_mode=pl.Buffered(1))
    in_specs = [pl.BlockSpec((8, 128), lambda bi, i: (0, 0)),
                pl.BlockSpec((1, HD_DIFF), lambda bi, i: (0, 0)),
                qspec, kspec, kspec]
    args = [lamp, gain, qs, kb, vb]
    scratch = [pltpu.VMEM((nt, rb * H_DIFF, V_ROWS, tk), bf16),
               pltpu.VMEM((rb * H_DIFF, V_ROWS, 2 * tqe), f32),
               pltpu.VMEM((rb * H_DIFF, 1, 2 * tqe), f32)]
    if past:
        pspec = pl.BlockSpec((1, rb, past * H_DIFF, HD_DIFF), lambda bi, i: (layer, bi, 0, 0),
                             pipeline_mode=pl.Buffered(1))
        in_specs += [pspec, pspec]
        args += [hist_k, hist_v]
    return pl.pallas_call(
        functools.partial(_attn_kernel, tq=tq, tk=tk, past=past, seq=l, rb=rb, lam_init=lam_init),
        grid=(b // rb, l // tq),
        in_specs=in_specs,
        out_specs=qspec,
        out_shape=jax.ShapeDtypeStruct((b, l, W_DIFF), bf16),
        scratch_shapes=scratch,
        compiler_params=_cparams(("arbitrary", "arbitrary")),
        name="attn",
    )(*args)


def _mix_kernel(x_ref, oa_ref, ob_ref, wo_ref, ln_ref, wq_ref, mk_ref, mv_ref, wmo_ref, h_ref):
    h1 = x_ref[0] + _mm(oa_ref[0], wo_ref[0:W_GDN, :]) + _mm(ob_ref[0], wo_ref[W_GDN:, :])
    hn = _rms(h1, ln_ref[...]).astype(bf16)
    q = (_mm(hn, wq_ref[...]) * (D_MEM ** -0.5 * LOG2E)).astype(bf16)
    parts = []
    for h in range(H_MEM):
        cs = slice(h * D_MEM, (h + 1) * D_MEM)
        s = _mm_nt(q[:, cs], mk_ref[0, :, cs])
        p = jnp.exp2(s - jnp.max(s, axis=-1, keepdims=True))
        den = jnp.sum(p, axis=-1, keepdims=True)
        parts.append(_mm(p.astype(bf16), mv_ref[0, :, cs]) / den)
    o = jnp.concatenate(parts, axis=-1).astype(bf16)
    h_ref[0] = h1 + _mm(o, wmo_ref[...])


def _mix(x, oa, ob, w_out, ln, wq, mk, mv, wmo):
    b, l, _ = x.shape
    tm = min(l, ROWS_MIX)
    n_mem = mk.shape[1]
    tok = lambda w: pl.BlockSpec((1, tm, w), lambda bi, i: (bi, i, 0))
    full = lambda s: pl.BlockSpec(s, lambda bi, i: (0,) * len(s), pipeline_mode=pl.Buffered(1))
    mspec = pl.BlockSpec((1, n_mem, D_MODEL), lambda bi, i: (bi, 0, 0))
    return pl.pallas_call(
        _mix_kernel,
        grid=(b, l // tm),
        in_specs=[tok(D_MODEL), tok(W_GDN), tok(W_DIFF), full((D_MODEL, D_MODEL)), full((1, D_MODEL)),
                  full((D_MODEL, D_MODEL)), mspec, mspec, full((D_MODEL, D_MODEL))],
        out_specs=tok(D_MODEL),
        out_shape=jax.ShapeDtypeStruct((b, l, D_MODEL), f32),
        compiler_params=_cparams(("arbitrary", "arbitrary")),
        name="mix",
    )(x, oa, ob, w_out, ln, wq, mk, mv, wmo)


def _ffn_kernel(h_ref, ln_ref, w1_ref, w2_ref, lnf_ref, o_ref, *, final):
    h = h_ref[...]
    hn = _rms(h, ln_ref[...]).astype(bf16)
    acc = h
    step = 1024
    for c in range(D_FF // step):
        a = jnp.maximum(_mm(hn, w1_ref[:, c * step:(c + 1) * step]), 0.0)
        acc = acc + _mm((a * a).astype(bf16), w2_ref[c * step:(c + 1) * step, :])
    if final:
        acc = _rms(acc, lnf_ref[...])
    o_ref[...] = acc


def _ffn(h2d, ln, w1, w2, lnf, final):
    t = h2d.shape[0]
    tm = min(t, ROWS_FFN)
    full = lambda s: pl.BlockSpec(s, lambda i: (0,) * len(s), pipeline_mode=pl.Buffered(1))
    tok = pl.BlockSpec((tm, D_MODEL), lambda i: (i, 0))
    return pl.pallas_call(
        functools.partial(_ffn_kernel, final=final),
        grid=(t // tm,),
        in_specs=[tok, full((1, D_MODEL)), full((D_MODEL, D_FF)), full((D_FF, D_MODEL)), full((1, D_MODEL))],
        out_specs=tok,
        out_shape=jax.ShapeDtypeStruct((t, D_MODEL), f32),
        compiler_params=_cparams(("arbitrary",)),
        name="ffn",
    )(h2d, ln, w1, w2, lnf)


def _pad_lanes(x, width=128):
    return jnp.pad(x, ((0, 0), (0, width - x.shape[-1])))


def kernel(x_prompt, x_sample, mem_prompt, cache_diff_k, cache_diff_v, cache_mem_k, cache_mem_v,
           state_gdn, state_gdn_conv, ln_mix, w_in, conv_w, a_log, dt_bias, gdn_norm,
           lambda_q1, lambda_k1, lambda_q2, lambda_k2, diff_norm, w_out, ln_mem_q, ln_mem_kv,
           w_mem_q, w_mem_k, w_mem_v, w_mem_o, ln_ffn, w_ff1, w_ff2, ln_final):
    depth = w_in.shape[0]
    bp, lp, _ = x_prompt.shape
    bs, ls, _ = x_sample.shape
    n_mem = mem_prompt.shape[1]
    past = cache_diff_k.shape[2]

    w_main = jnp.concatenate([w_in[:, :, :AB_OFF], w_in[:, :, AB_OFF + 2 * H_GDN:]], axis=2).astype(bf16)
    w_ab = jnp.pad(w_in[:, :, AB_OFF:AB_OFF + 2 * H_GDN], ((0, 0), (0, 0), (0, 128 - 2 * H_GDN))).astype(bf16)
    w_out_b, wq_b, wk_b, wv_b, wmo_b = (w.astype(bf16) for w in (w_out, w_mem_q, w_mem_k, w_mem_v, w_mem_o))
    w1_b, w2_b = w_ff1.astype(bf16), w_ff2.astype(bf16)
    alog_p = _pad_lanes(a_log)
    dtb_p = _pad_lanes(dt_bias)
    lamp = jnp.stack([_pad_lanes(p) for p in (lambda_q1, lambda_k1, lambda_q2, lambda_k2)], axis=1)
    lamp = jnp.pad(lamp, ((0, 0), (0, 4), (0, 0)))
    lnf = ln_final.reshape(1, D_MODEL)

    mk_all, mv_all, mkb_all, mvb_all = _memkv(mem_prompt.reshape(bp * n_mem, D_MODEL),
                                              ln_mem_kv.reshape(depth, 1, D_MODEL), wk_b, wv_b)

    cbuf_p = jnp.zeros((bp, 8, CONV_CH), f32)
    s0_p = jnp.zeros((bp, H_GDN, DK_GDN, DV_GDN), f32)
    cbuf_s_all = jnp.pad(state_gdn_conv, ((0, 0), (0, 0), (8 - (CONV_W - 1), 0), (0, 0)))
    cmk_b = cache_mem_k.reshape(depth, bs, n_mem, D_MODEL).astype(bf16)
    cmv_b = cache_mem_v.reshape(depth, bs, n_mem, D_MODEL).astype(bf16)
    hist_k = cache_diff_k.reshape(depth, bs, past * H_DIFF, HD_DIFF)
    hist_v = cache_diff_v.reshape(depth, bs, past * H_DIFF, HD_DIFF)

    def layer(l, x, cbuf8, s0, hk, hv, mk, mv, kd_all, vd_all, final):
        b, ln_, _ = x.shape
        lam_init = 0.8 - 0.6 * math.exp(-0.3 * l)
        cqkv, z, ab, qs, kd_all, vd_all, kb, vb, cnew = _inproj(
            x, ln_mix[l].reshape(1, D_MODEL), w_main[l], w_ab[l], conv_w[l],
            alog_p[l:l + 1], dtb_p[l:l + 1], cbuf8, l, depth, kd_all, vd_all)
        oa, s_new = _gdn(cqkv, z, ab, s0, gdn_norm[l].reshape(1, DV_GDN))
        ob = _attn(lamp[l], diff_norm[l].reshape(1, HD_DIFF), qs, kb, vb, hk, hv, l, lam_init)
        h = _mix(x, oa, ob, w_out_b[l], ln_mem_q[l].reshape(1, D_MODEL), wq_b[l], mk, mv, wmo_b[l])
        h = _ffn(h.reshape(b * ln_, D_MODEL), ln_ffn[l].reshape(1, D_MODEL), w1_b[l], w2_b[l], lnf, final)
        return h.reshape(b, ln_, D_MODEL), kd_all, vd_all, s_new, cnew[:, 8 - (CONV_W - 1):, :]

    hp, hs = x_prompt, x_sample
    pk_ = pv_ = sk_ = sv_ = None
    ps_, pc_, ss_, sc_ = [], [], [], []
    for l in range(depth):
        final = l == depth - 1
        hp, pk_, pv_, s_, c_ = layer(l, hp, cbuf_p, s0_p, None, None,
                                     mkb_all[l].reshape(bp, n_mem, D_MODEL),
                                     mvb_all[l].reshape(bp, n_mem, D_MODEL), pk_, pv_, final)
        ps_.append(s_); pc_.append(c_)
        hs, sk_, sv_, s_, c_ = layer(l, hs, cbuf_s_all[l], state_gdn[l], hist_k, hist_v,
                                     cmk_b[l], cmv_b[l], sk_, sv_, final)
        ss_.append(s_); sc_.append(c_)

    mem_shape = (depth, bp, n_mem, H_MEM, D_MEM)
    return (hp, hs,
            pk_.reshape(depth, bp, lp, H_DIFF, HD_DIFF), pv_.reshape(depth, bp, lp, H_DIFF, HD_DIFF),
            jnp.stack(ps_), jnp.stack(pc_),
            mk_all.reshape(mem_shape), mv_all.reshape(mem_shape),
            sk_.reshape(depth, bs, ls, H_DIFF, HD_DIFF), sv_.reshape(depth, bs, ls, H_DIFF, HD_DIFF),
            jnp.stack(ss_), jnp.stack(sc_))
```

```python
import functools
import math

import jax
import jax.numpy as jnp
from jax import lax
from jax.experimental import pallas as pl
from jax.experimental.pallas import tpu as pltpu

D_MODEL = 1024
CHUNK = 64
H_GDN = 4
DK_GDN = 128
DV_GDN = 128
CONV_W = 4
W_GDN = H_GDN * DK_GDN
CONV_CH = 3 * W_GDN
H_DIFF = 4
D_DIFF = 64
HD_DIFF = 2 * D_DIFF
W_DIFF = H_DIFF * HD_DIFF
H_MEM = 4
D_MEM = 256
D_FF = 4 * D_MODEL
EPS = 1e-6
LOG2E = 1.4426950408889634
NEG = -1e30
MAIN_COLS = CONV_CH + 4 * 512
AB_OFF = CONV_CH + W_GDN
ROWS_GDN = H_GDN * CHUNK
V_ROWS = HD_DIFF + 16
ATT_TILE = 256
ROWS_MEMKV = 512
ROWS_INPROJ = 512
ROWS_GDN = 256
ROWS_MIX = 1024
ROWS_FFN = 1024
BATCH_ROWS_GDN = 4
BATCH_ROWS_ATT = 2
VMEM_LIMIT = 56 * 1024 * 1024

f32 = jnp.float32
bf16 = jnp.bfloat16


def _cparams(sem):
    return pltpu.CompilerParams(dimension_semantics=sem, vmem_limit_bytes=VMEM_LIMIT)


def _rms(x, g):
    ms = jnp.mean(x * x, axis=-1, keepdims=True)
    return x * lax.rsqrt(ms + EPS) * g


def _mm(a, b):
    return jnp.dot(a, b, preferred_element_type=f32)


def _mm_nt(a, b):
    return lax.dot_general(a, b, (((1,), (1,)), ((), ())), preferred_element_type=f32)


def _memkv_kernel(mem_ref, g_ref, wk_ref, wv_ref, mk_ref, mv_ref, mkb_ref, mvb_ref):
    xn = _rms(mem_ref[...], g_ref[0]).astype(bf16)
    k = _mm(xn, wk_ref[0])
    v = _mm(xn, wv_ref[0])
    mk_ref[0] = k
    mv_ref[0] = v
    mkb_ref[0] = k.astype(bf16)
    mvb_ref[0] = v.astype(bf16)


def _memkv(mem2d, ln, wk, wv):
    depth = wk.shape[0]
    t = mem2d.shape[0]
    tm = min(t, ROWS_MEMKV)
    assert t % tm == 0, (t, tm)
    out = jax.ShapeDtypeStruct((depth, t, D_MODEL), f32)
    outb = jax.ShapeDtypeStruct((depth, t, D_MODEL), bf16)
    wspec = pl.BlockSpec((1, D_MODEL, D_MODEL), lambda l, i: (l, 0, 0))
    ospec = pl.BlockSpec((1, tm, D_MODEL), lambda l, i: (l, i, 0))
    return pl.pallas_call(
        _memkv_kernel,
        grid=(depth, t // tm),
        in_specs=[pl.BlockSpec((tm, D_MODEL), lambda l, i: (i, 0)),
                  pl.BlockSpec((1, 1, D_MODEL), lambda l, i: (l, 0, 0)),
                  wspec, wspec],
        out_specs=[ospec, ospec, ospec, ospec],
        out_shape=[out, out, outb, outb],
        compiler_params=_cparams(("arbitrary", "arbitrary")),
        name="memkv",
    )(mem2d, ln, wk, wv)


def _inproj_kernel(*refs, tm, aliased):
    if aliased:
        refs = refs[:8] + refs[10:]
    (x_ref, ln_ref, w_ref, wab_ref, cw_ref, alog_ref, dtb_ref, cbuf_ref,
     cqkv_ref, z_ref, ab_ref, qs_ref, kd_ref, vd_ref, kb_ref, vb_ref, cnew_ref, cs_ref) = refs
    i = pl.program_id(1)

    @pl.when(i == 0)
    def _():
        cs_ref[0:8, :] = cbuf_ref[0]

    xn = _rms(x_ref[0], ln_ref[...]).astype(bf16)
    conv_in = _mm(xn, w_ref[:, 0:CONV_CH])
    cs_ref[8:8 + tm, :] = conv_in
    for blk in range(CONV_CH // 128):
        cols = slice(blk * 128, (blk + 1) * 128)
        y = cw_ref[0:1, cols] * cs_ref[5:5 + tm, cols]
        y = y + cw_ref[1:2, cols] * cs_ref[6:6 + tm, cols]
        y = y + cw_ref[2:3, cols] * cs_ref[7:7 + tm, cols]
        y = y + cw_ref[3:4, cols] * cs_ref[8:8 + tm, cols]
        c = y * jax.nn.sigmoid(y)
        if blk < 2 * H_GDN:
            inv = lax.rsqrt(jnp.sum(c * c, axis=-1, keepdims=True) + EPS)
            if blk < H_GDN:
                inv = inv * DK_GDN ** -0.5
            c = c * inv
        cqkv_ref[0, :, cols] = c
    tail = cs_ref[tm:tm + 8, :]
    cnew_ref[0] = tail
    cs_ref[0:8, :] = tail

    z_ref[0] = _mm(xn, w_ref[:, CONV_CH:CONV_CH + 512])
    qd = _mm(xn, w_ref[:, CONV_CH + 512:CONV_CH + 1024])
    qs_ref[0] = (qd * (D_DIFF ** -0.5 * LOG2E)).astype(bf16)
    kd = _mm(xn, w_ref[:, CONV_CH + 1024:CONV_CH + 1536])
    kb_ref[0] = kd.astype(bf16)
    vd = _mm(xn, w_ref[:, CONV_CH + 1536:CONV_CH + 2048])
    vb_ref[0] = vd.astype(bf16)
    for h in range(H_DIFF):
        kd_ref[0, 0, pl.ds(h, tm, stride=H_DIFF), :] = kd[:, h * HD_DIFF:(h + 1) * HD_DIFF]
        vd_ref[0, 0, pl.ds(h, tm, stride=H_DIFF), :] = vd[:, h * HD_DIFF:(h + 1) * HD_DIFF]

    ab = _mm(xn, wab_ref[...])
    sp_in = ab + dtb_ref[...]
    softplus = jnp.maximum(sp_in, 0.0) + jnp.log1p(jnp.exp(-jnp.abs(sp_in)))
    g = -jnp.exp(alog_ref[...]) * softplus
    beta = jax.nn.sigmoid(ab)
    lane = lax.broadcasted_iota(jnp.int32, ab.shape, 1)
    ab_ref[0] = jnp.where(lane < H_GDN, g, beta)


def _inproj(x, ln, w_main, w_ab, conv_w, alog, dtb, cbuf8, layer, depth, kd_all, vd_all):
    b, l, _ = x.shape
    tm = min(l, ROWS_INPROJ)
    assert l % tm == 0 and tm % 8 == 0, (l, tm)
    nt = l // tm
    aliased = kd_all is not None
    tok = lambda w: pl.BlockSpec((1, tm, w), lambda bi, i: (bi, i, 0))
    full = lambda s: pl.BlockSpec(s, lambda bi, i: (0,) * len(s), pipeline_mode=pl.Buffered(1))
    cache = jax.ShapeDtypeStruct((depth, b, l * H_DIFF, HD_DIFF), f32)
    cache_spec = pl.BlockSpec((1, 1, tm * H_DIFF, HD_DIFF), lambda bi, i: (layer, bi, i, 0))
    out_shape = [
        jax.ShapeDtypeStruct((b, l, CONV_CH), f32),
        jax.ShapeDtypeStruct((b, l, W_GDN), f32),
        jax.ShapeDtypeStruct((b, l, 128), f32),
        jax.ShapeDtypeStruct((b, l, W_DIFF), bf16),
        cache, cache,
        jax.ShapeDtypeStruct((b, l, W_DIFF), bf16),
        jax.ShapeDtypeStruct((b, l, W_DIFF), bf16),
        jax.ShapeDtypeStruct((b, 8, CONV_CH), f32),
    ]
    out_specs = [tok(CONV_CH), tok(W_GDN), tok(128), tok(W_DIFF), cache_spec, cache_spec,
                 tok(W_DIFF), tok(W_DIFF),
                 pl.BlockSpec((1, 8, CONV_CH), lambda bi, i: (bi, 0, 0))]
    in_specs = [tok(D_MODEL), full((1, D_MODEL)), full((D_MODEL, MAIN_COLS)), full((D_MODEL, 128)),
                full((CONV_W, CONV_CH)), full((1, 128)), full((1, 128)),
                pl.BlockSpec((1, 8, CONV_CH), lambda bi, i: (bi, 0, 0))]
    args = [x, ln, w_main, w_ab, conv_w, alog, dtb, cbuf8]
    aliases = {}
    if aliased:
        in_specs += [pl.BlockSpec(memory_space=pl.ANY), pl.BlockSpec(memory_space=pl.ANY)]
        args += [kd_all, vd_all]
        aliases = {8: 4, 9: 5}
    return pl.pallas_call(
        functools.partial(_inproj_kernel, tm=tm, aliased=aliased),
        grid=(b, nt),
        in_specs=in_specs,
        out_specs=out_specs,
        out_shape=out_shape,
        input_output_aliases=aliases,
        scratch_shapes=[pltpu.VMEM((tm + 8, CONV_CH), f32)],
        compiler_params=_cparams(("arbitrary", "arbitrary")),
        name="inproj",
    )(*args)


def _split3(x):
    hi = x.astype(bf16)
    r1 = x - hi.astype(f32)
    mid = r1.astype(bf16)
    lo = (r1 - mid.astype(f32)).astype(bf16)
    return hi, mid, lo


def _mmb(a, b):
    return _mm(a.astype(bf16), b.astype(bf16))


def _unit_lower_inverses(mats, row, col):
    n_rows = row.shape[0]

    def blk(n):
        sh = n.bit_length() - 1
        return (row >> sh) == (col >> sh)

    eye = (row == col).astype(f32)
    a8 = [jnp.where(blk(8), a, 0.0) for a in mats]
    p = [_mmb(x, x) for x in a8]
    q = [_mmb(x, x) for x in p]
    t = [_mmb(eye - x, eye + y) for x, y in zip(a8, p)]
    t = [_mmb(x, eye + y) for x, y in zip(t, q)]
    for n in (8, 16, 32):
        sel = jnp.logical_and(blk(2 * n), jnp.logical_not(blk(n)))
        off = [jnp.where(sel, a, 0.0).astype(bf16) for a in mats]
        starts = range(0, n_rows, 2 * n)
        low = [jnp.concatenate([ti[r + n:r + 2 * n] for r in starts], axis=0) for ti in t]
        x = [_mm(li.astype(bf16), oi) for li, oi in zip(low, off)]
        low = [li - _mmb(xi, ti) for li, xi, ti in zip(low, x, t)]
        t = [jnp.concatenate([piece for k, r in enumerate(starts)
                              for piece in (ti[r:r + n], li[k * n:(k + 1) * n])], axis=0)
             for ti, li in zip(t, low)]
    return t


def _stack_heads(x):
    return jnp.concatenate([x[:, h * 128:(h + 1) * 128] for h in range(H_GDN)], axis=0)


def _gdn_kernel(cqkv_ref, ab_ref, z_ref, s0_ref, gn_ref, oa_ref, s_ref, *, nc, rb):
    i = pl.program_id(1)

    @pl.when(i == 0)
    def _():
        s_ref[...] = s0_ref[...]

    n = ROWS_GDN
    row = lax.broadcasted_iota(jnp.int32, (n, n), 0)
    col = lax.broadcasted_iota(jnp.int32, (n, n), 1)
    same = (row >> 6) == (col >> 6)
    lower = jnp.logical_and(same, row >= col)
    strict = jnp.logical_and(same, row > col)
    r64 = lax.broadcasted_iota(jnp.int32, (CHUNK, CHUNK), 0)
    c64 = lax.broadcasted_iota(jnp.int32, (CHUNK, CHUNK), 1)
    tril64 = (r64 >= c64).astype(bf16)
    head_of_col = lax.broadcasted_iota(jnp.int32, (DK_GDN, n), 1) >> 6
    heads = range(H_GDN)
    hr = [slice(h * CHUNK, (h + 1) * CHUNK) for h in heads]

    def bcast_col(m, lane):
        return jnp.concatenate(
            [jnp.broadcast_to(m[:, lane + h:lane + h + 1], (m.shape[0], 128)) for h in heads], axis=0)

    items = [(r, slice(c * CHUNK, (c + 1) * CHUNK)) for r in range(rb) for c in range(nc)]

    gb = [ab_ref[r, t, :] for r, t in items]
    parts = [_split3(x) for x in gb]
    gcum = [_mm(tril64, hi) + _mm(tril64, mid) + _mm(tril64, lo) for hi, mid, lo in parts]
    gs = [bcast_col(x, 0) for x in gcum]
    bs = [bcast_col(x, H_GDN) for x in gb]
    glast = [x[CHUNK - 1:CHUNK, :] for x in gcum]
    gl = [bcast_col(jnp.broadcast_to(x, (CHUNK, 128)), 0) for x in glast]
    grow = [x.T[0:1, :] for x in gs]
    gam = [jnp.where(lower, jnp.exp(jnp.where(lower, jnp.concatenate([x, x], axis=1) - y, 0.0)), 0.0)
           for x, y in zip(gs, grow)]
    ks = [_stack_heads(cqkv_ref[r, t, W_GDN:2 * W_GDN]) for r, t in items]
    qs = [_stack_heads(cqkv_ref[r, t, 0:W_GDN]) for r, t in items]
    vs = [_stack_heads(cqkv_ref[r, t, 2 * W_GDN:CONV_CH]) for r, t in items]
    kq = [_mm_nt(jnp.concatenate([k, q], axis=0).astype(bf16), k.astype(bf16)) for k, q in zip(ks, qs)]
    a = [jnp.where(strict, jnp.concatenate([b_, b_], axis=1) * x[0:n] * g, 0.0) for b_, x, g in zip(bs, kq, gam)]
    qk = [(x[n:2 * n] * g).astype(bf16) for x, g in zip(kq, gam)]
    t_inv = _unit_lower_inverses(a, row, col)
    eg = [jnp.exp(x) for x in gs]
    uw = [_mmb(ti, jnp.concatenate([b_ * v, b_ * e * k], axis=1))
          for ti, b_, v, e, k in zip(t_inv, bs, vs, eg, ks)]
    wq = [[jnp.concatenate([x[hr[h], DV_GDN:], (q * e)[hr[h]]], axis=0).astype(bf16) for h in heads]
          for x, q, e in zip(uw, qs, eg)]
    kdec_t = [(k * jnp.exp(l_ - g)).T for k, l_, g in zip(ks, gl, gs)]
    kdl = [jnp.concatenate([jnp.where(head_of_col == h, x, 0.0) for h in heads], axis=0).astype(bf16)
           for x in kdec_t]
    decay = [jnp.exp(x) for x in glast]
    decay = [jnp.concatenate([jnp.broadcast_to(x[:, h:h + 1], (DK_GDN, DV_GDN)) for h in heads], axis=0)
             for x in decay]
    gate = [_stack_heads(z_ref[r, t, :]) for r, t in items]
    gate = [x * jax.nn.sigmoid(x) for x in gate]

    s_cur = [s_ref[r].reshape(H_GDN * DK_GDN, DV_GDN) for r in range(rb)]
    for c in range(nc):
        idx = [r * nc + c for r in range(rb)]
        r1 = [[_mm(wq[k][h], s_cur[r][h * DK_GDN:(h + 1) * DK_GDN].astype(bf16)) for h in heads]
              for r, k in enumerate(idx)]
        vn = [(uw[k][:, 0:DV_GDN] - jnp.concatenate([r1[r][h][0:CHUNK] for h in heads], axis=0)).astype(bf16)
              for r, k in enumerate(idx)]
        s_cur = [decay[k] * s_cur[r] + _mm(kdl[k], vn[r]) for r, k in enumerate(idx)]
        o = [jnp.concatenate([r1[r][h][CHUNK:] for h in heads], axis=0) + _mm(qk[k], vn[r])
             for r, k in enumerate(idx)]
        for r, k in enumerate(idx):
            out = (_rms(o[r], gn_ref[...]) * gate[k]).astype(bf16)
            for h in heads:
                oa_ref[r, items[k][1], h * 128:(h + 1) * 128] = out[hr[h]]
    for r in range(rb):
        s_ref[r] = s_cur[r].reshape(H_GDN, DK_GDN, DV_GDN)


def _gdn(cqkv, z, ab, s0, gn):
    b, l, _ = cqkv.shape
    tg = min(l, ROWS_GDN)
    assert l % tg == 0 and tg % CHUNK == 0, (l, tg)
    rb = BATCH_ROWS_GDN if b % BATCH_ROWS_GDN == 0 else 1
    tok = lambda w: pl.BlockSpec((rb, tg, w), lambda bi, i: (bi, i, 0))
    sspec = pl.BlockSpec((rb, H_GDN, DK_GDN, DV_GDN), lambda bi, i: (bi, 0, 0, 0))
    return pl.pallas_call(
        functools.partial(_gdn_kernel, nc=tg // CHUNK, rb=rb),
        grid=(b // rb, l // tg),
        in_specs=[tok(CONV_CH), tok(128), tok(W_GDN), sspec,
                  pl.BlockSpec((1, DV_GDN), lambda bi, i: (0, 0))],
        out_specs=[tok(W_GDN), sspec],
        out_shape=[jax.ShapeDtypeStruct((b, l, W_GDN), bf16),
                   jax.ShapeDtypeStruct((b, H_GDN, DK_GDN, DV_GDN), f32)],
        compiler_params=_cparams(("arbitrary", "arbitrary")),
        name="gdn",
    )(cqkv, ab, z, s0, gn)


def _attn_kernel(*refs, tq, tk, past, seq, rb, lam_init):
    if past:
        lam_ref, gain_ref, q_ref, k_ref, v_ref, pk_ref, pv_ref, o_ref, vt_ref, acc_ref, m_ref = refs
    else:
        lam_ref, gain_ref, q_ref, k_ref, v_ref, o_ref, vt_ref, acc_ref, m_ref = refs
    i = pl.program_id(1)
    npast = past // tk
    tqe = max(tq, 128)
    heads = range(rb * H_DIFF)
    row_of = [u // H_DIFF for u in heads]
    head_of = [u % H_DIFF for u in heads]
    hs = [slice(head_of[u] * HD_DIFF, (head_of[u] + 1) * HD_DIFF) for u in heads]

    def pad_rows(x):
        if x.shape[0] == tk:
            return x
        return jnp.concatenate([x, jnp.zeros((tk - x.shape[0], x.shape[1]), x.dtype)], axis=0)

    def history(ref, t, u):
        return ref[0, row_of[u], pl.ds(t * tk * H_DIFF + head_of[u], tk, stride=H_DIFF), :]

    @pl.when(i == 0)
    def _build():
        ones = jnp.ones((V_ROWS - HD_DIFF, tk), bf16)

        def put(t, u, vtile):
            vt_ref[t, u, 0:HD_DIFF, :] = vtile.astype(f32).T.astype(bf16)
            vt_ref[t, u, HD_DIFF:V_ROWS, :] = ones

        for t in range(npast):
            for u in heads:
                put(t, u, history(pv_ref, t, u))
        for t in range(max(seq // tk, 1)):
            for u in heads:
                put(npast + t, u, pad_rows(v_ref[row_of[u], t * tk:min((t + 1) * tk, seq), hs[u]]))

    def padded_q(u):
        qu = q_ref[row_of[u], :, hs[u]]
        if tqe > tq:
            qu = jnp.concatenate([qu, jnp.zeros((tqe - tq, HD_DIFF), bf16)], axis=0)
        return qu

    lane = lax.broadcasted_iota(jnp.int32, (tqe, HD_DIFF), 1)
    zero = jnp.zeros((tqe, HD_DIFF), bf16)
    qps = [jnp.concatenate([jnp.where(lane < D_DIFF, padded_q(u), zero),
                            jnp.where(lane >= D_DIFF, padded_q(u), zero)], axis=0) for u in heads]

    acc_ref[...] = jnp.zeros_like(acc_ref)
    m_ref[...] = jnp.full(m_ref.shape, NEG, f32)

    def scores(kts):
        return [_mm_nt(kts[h], qps[h]) for h in heads]

    def process(tiles):
        sts = []
        for _, st, masked in tiles:
            if masked:
                krow = lax.broadcasted_iota(jnp.int32, (tk, 2 * tqe), 0)
                qcol = lax.broadcasted_iota(jnp.int32, (tk, 2 * tqe), 1)
                qcol = jnp.where(qcol >= tqe, qcol - tqe, qcol)
                visible = (krow >> 6) <= (qcol >> 6)
                st = [jnp.where(visible, x, NEG) for x in st]
            sts.append(st)
        m_olds = [m_ref[h] for h in heads]
        m_news = m_olds
        for st in sts:
            m_news = [jnp.maximum(m_news[h], jnp.max(st[h], axis=0, keepdims=True)) for h in heads]
        alphas = [jnp.exp2(m_olds[h] - m_news[h]) for h in heads]
        pvs = None
        for (j, _, _), st in zip(tiles, sts):
            ps = [jnp.exp2(st[h] - m_news[h]).astype(bf16) for h in heads]
            pv = [_mm(vt_ref[j, h], ps[h]) for h in heads]
            pvs = pv if pvs is None else [pvs[h] + pv[h] for h in heads]
        for h in heads:
            acc_ref[h] = alphas[h] * acc_ref[h] + pvs[h]
            m_ref[h] = m_news[h]

    if past:
        diag_keys = [pad_rows(k_ref[row_of[u], :, hs[u]]) for u in heads]
        keys = [[history(pk_ref, t, u).astype(bf16) for u in heads] for t in range(npast)] + [diag_keys]
        for j in range(0, npast + 1, 2):
            process([(t, scores(keys[t]), t == npast) for t in range(j, min(j + 2, npast + 1))])
    else:
        def keys(j):
            rows = pl.ds(pl.multiple_of(j * tk, tk), tk)
            return [k_ref[row_of[u], rows, hs[u]] for u in heads]

        def body(jj, carry):
            j = 2 * jj
            process([(j, scores(keys(j)), False), (j + 1, scores(keys(j + 1)), False)])
            return carry

        lax.fori_loop(0, i // 2, body, 0)
        odd = lax.rem(i, 2) == 1

        @pl.when(odd)
        def _():
            process([(i - 1, scores(keys(i - 1)), False), (i, scores(keys(i)), True)])

        @pl.when(jnp.logical_not(odd))
        def _():
            process([(i, scores(keys(i)), True)])

    lam = (jnp.exp(jnp.sum(lam_ref[0:1, :] * lam_ref[1:2, :], axis=-1, keepdims=True))
           - jnp.exp(jnp.sum(lam_ref[2:3, :] * lam_ref[3:4, :], axis=-1, keepdims=True)) + lam_init)
    for h in heads:
        acc = acc_ref[h]
        o0 = acc[0:HD_DIFF, 0:tqe] / acc[HD_DIFF:HD_DIFF + 1, 0:tqe]
        o1 = acc[0:HD_DIFF, tqe:] / acc[HD_DIFF:HD_DIFF + 1, tqe:]
        ot = o0 - lam * o1
        ms = jnp.mean(ot * ot, axis=0, keepdims=True)
        o = (ot * lax.rsqrt(ms + EPS)).T * gain_ref[...] * (1.0 - lam_init)
        o_ref[row_of[h], :, hs[h]] = o[0:tq].astype(bf16)


def _attn(lamp, gain, qs, kb, vb, hist_k, hist_v, layer, lam_init):
    b, l, _ = qs.shape
    tk = ATT_TILE
    tq = min(l, tk)
    past = 0 if hist_k is None else hist_k.shape[2] // H_DIFF
    rb = BATCH_ROWS_ATT if (b % BATCH_ROWS_ATT == 0 and past == 0) else 1
    assert past % tk == 0 and (l % tk == 0 or l < tk)
    assert past == 0 or l <= tk
    nt = past // tk + max(l // tk, 1)
    tqe = max(tq, 128)
    qspec = pl.BlockSpec((rb, tq, W_DIFF), lambda bi, i: (bi, i, 0))
    kspec = pl.BlockSpec((rb, l, W_DIFF), lambda bi, i: (bi, 0, 0), pipeline_mode=pl.Buffered(1))
    in_specs = [pl.BlockSpec((8, 128), lambda bi, i: (0, 0)),
                pl.BlockSpec((1, HD_DIFF), lambda bi, i: (0, 0)),
                qspec, kspec, kspec]
    args = [lamp, gain, qs, kb, vb]
    scratch = [pltpu.VMEM((nt, rb * H_DIFF, V_ROWS, tk), bf16),
               pltpu.VMEM((rb * H_DIFF, V_ROWS, 2 * tqe), f32),
               pltpu.VMEM((rb * H_DIFF, 1, 2 * tqe), f32)]
    if past:
        pspec = pl.BlockSpec((1, rb, past * H_DIFF, HD_DIFF), lambda bi, i: (layer, bi, 0, 0),
                             pipeline_mode=pl.Buffered(1))
        in_specs += [pspec, pspec]
        args += [hist_k, hist_v]
    return pl.pallas_call(
        functools.partial(_attn_kernel, tq=tq, tk=tk, past=past, seq=l, rb=rb, lam_init=lam_init),
        grid=(b // rb, l // tq),
        in_specs=in_specs,
        out_specs=qspec,
        out_shape=jax.ShapeDtypeStruct((b, l, W_DIFF), bf16),
        scratch_shapes=scratch,
        compiler_params=_cparams(("arbitrary", "arbitrary")),
        name="attn",
    )(*args)


def _mix_kernel(x_ref, oa_ref, ob_ref, wo_ref, ln_ref, wq_ref, mk_ref, mv_ref, wmo_ref, h_ref):
    h1 = x_ref[0] + _mm(oa_ref[0], wo_ref[0:W_GDN, :]) + _mm(ob_ref[0], wo_ref[W_GDN:, :])
    hn = _rms(h1, ln_ref[...]).astype(bf16)
    q = (_mm(hn, wq_ref[...]) * (D_MEM ** -0.5 * LOG2E)).astype(bf16)
    parts = []
    for h in range(H_MEM):
        cs = slice(h * D_MEM, (h + 1) * D_MEM)
        s = _mm_nt(q[:, cs], mk_ref[0, :, cs])
        p = jnp.exp2(s - jnp.max(s, axis=-1, keepdims=True))
        den = jnp.sum(p, axis=-1, keepdims=True)
        parts.append(_mm(p.astype(bf16), mv_ref[0, :, cs]) / den)
    o = jnp.concatenate(parts, axis=-1).astype(bf16)
    h_ref[0] = h1 + _mm(o, wmo_ref[...])


def _mix(x, oa, ob, w_out, ln, wq, mk, mv, wmo):
    b, l, _ = x.shape
    tm = min(l, ROWS_MIX)
    assert l % tm == 0, (l, tm)
    n_mem = mk.shape[1]
    tok = lambda w: pl.BlockSpec((1, tm, w), lambda bi, i: (bi, i, 0))
    full = lambda s: pl.BlockSpec(s, lambda bi, i: (0,) * len(s), pipeline_mode=pl.Buffered(1))
    mspec = pl.BlockSpec((1, n_mem, D_MODEL), lambda bi, i: (bi, 0, 0))
    return pl.pallas_call(
        _mix_kernel,
        grid=(b, l // tm),
        in_specs=[tok(D_MODEL), tok(W_GDN), tok(W_DIFF), full((D_MODEL, D_MODEL)), full((1, D_MODEL)),
                  full((D_MODEL, D_MODEL)), mspec, mspec, full((D_MODEL, D_MODEL))],
        out_specs=tok(D_MODEL),
        out_shape=jax.ShapeDtypeStruct((b, l, D_MODEL), f32),
        compiler_params=_cparams(("arbitrary", "arbitrary")),
        name="mix",
    )(x, oa, ob, w_out, ln, wq, mk, mv, wmo)


def _ffn_kernel(h_ref, ln_ref, w1_ref, w2_ref, lnf_ref, o_ref, *, final):
    h = h_ref[...]
    hn = _rms(h, ln_ref[...]).astype(bf16)
    acc = h
    step = 1024
    for c in range(D_FF // step):
        a = jnp.maximum(_mm(hn, w1_ref[:, c * step:(c + 1) * step]), 0.0)
        acc = acc + _mm((a * a).astype(bf16), w2_ref[c * step:(c + 1) * step, :])
    if final:
        acc = _rms(acc, lnf_ref[...])
    o_ref[...] = acc


def _ffn(h2d, ln, w1, w2, lnf, final):
    t = h2d.shape[0]
    tm = min(t, ROWS_FFN)
    assert t % tm == 0, (t, tm)
    full = lambda s: pl.BlockSpec(s, lambda i: (0,) * len(s), pipeline_mode=pl.Buffered(1))
    tok = pl.BlockSpec((tm, D_MODEL), lambda i: (i, 0))
    return pl.pallas_call(
        functools.partial(_ffn_kernel, final=final),
        grid=(t // tm,),
        in_specs=[tok, full((1, D_MODEL)), full((D_MODEL, D_FF)), full((D_FF, D_MODEL)), full((1, D_MODEL))],
        out_specs=tok,
        out_shape=jax.ShapeDtypeStruct((t, D_MODEL), f32),
        compiler_params=_cparams(("arbitrary",)),
        name="ffn",
    )(h2d, ln, w1, w2, lnf)


def _pad_lanes(x, width=128):
    return jnp.pad(x, ((0, 0), (0, width - x.shape[-1])))


def kernel(x_prompt, x_sample, mem_prompt, cache_diff_k, cache_diff_v, cache_mem_k, cache_mem_v,
           state_gdn, state_gdn_conv, ln_mix, w_in, conv_w, a_log, dt_bias, gdn_norm,
           lambda_q1, lambda_k1, lambda_q2, lambda_k2, diff_norm, w_out, ln_mem_q, ln_mem_kv,
           w_mem_q, w_mem_k, w_mem_v, w_mem_o, ln_ffn, w_ff1, w_ff2, ln_final):
    depth = w_in.shape[0]
    bp, lp, _ = x_prompt.shape
    bs, ls, _ = x_sample.shape
    n_mem = mem_prompt.shape[1]
    past = cache_diff_k.shape[2]

    w_main = jnp.concatenate([w_in[:, :, :AB_OFF], w_in[:, :, AB_OFF + 2 * H_GDN:]], axis=2).astype(bf16)
    w_ab = jnp.pad(w_in[:, :, AB_OFF:AB_OFF + 2 * H_GDN], ((0, 0), (0, 0), (0, 128 - 2 * H_GDN))).astype(bf16)
    w_out_b, wq_b, wk_b, wv_b, wmo_b = (w.astype(bf16) for w in (w_out, w_mem_q, w_mem_k, w_mem_v, w_mem_o))
    w1_b, w2_b = w_ff1.astype(bf16), w_ff2.astype(bf16)
    alog_p = _pad_lanes(a_log)
    dtb_p = _pad_lanes(dt_bias)
    lamp = jnp.stack([_pad_lanes(p) for p in (lambda_q1, lambda_k1, lambda_q2, lambda_k2)], axis=1)
    lamp = jnp.pad(lamp, ((0, 0), (0, 4), (0, 0)))
    lnf = ln_final.reshape(1, D_MODEL)

    mk_all, mv_all, mkb_all, mvb_all = _memkv(mem_prompt.reshape(bp * n_mem, D_MODEL),
                                              ln_mem_kv.reshape(depth, 1, D_MODEL), wk_b, wv_b)

    cbuf_p = jnp.zeros((bp, 8, CONV_CH), f32)
    s0_p = jnp.zeros((bp, H_GDN, DK_GDN, DV_GDN), f32)
    cbuf_s_all = jnp.pad(state_gdn_conv, ((0, 0), (0, 0), (8 - (CONV_W - 1), 0), (0, 0)))
    cmk_b = cache_mem_k.reshape(depth, bs, n_mem, D_MODEL).astype(bf16)
    cmv_b = cache_mem_v.reshape(depth, bs, n_mem, D_MODEL).astype(bf16)
    hist_k = cache_diff_k.reshape(depth, bs, past * H_DIFF, HD_DIFF)
    hist_v = cache_diff_v.reshape(depth, bs, past * H_DIFF, HD_DIFF)

    def layer(l, x, cbuf8, s0, hk, hv, mk, mv, kd_all, vd_all, final):
        b, ln_, _ = x.shape
        lam_init = 0.8 - 0.6 * math.exp(-0.3 * l)
        cqkv, z, ab, qs, kd_all, vd_all, kb, vb, cnew = _inproj(
            x, ln_mix[l].reshape(1, D_MODEL), w_main[l], w_ab[l], conv_w[l],
            alog_p[l:l + 1], dtb_p[l:l + 1], cbuf8, l, depth, kd_all, vd_all)
        oa, s_new = _gdn(cqkv, z, ab, s0, gdn_norm[l].reshape(1, DV_GDN))
        ob = _attn(lamp[l], diff_norm[l].reshape(1, HD_DIFF), qs, kb, vb, hk, hv, l, lam_init)
        h = _mix(x, oa, ob, w_out_b[l], ln_mem_q[l].reshape(1, D_MODEL), wq_b[l], mk, mv, wmo_b[l])
        h = _ffn(h.reshape(b * ln_, D_MODEL), ln_ffn[l].reshape(1, D_MODEL), w1_b[l], w2_b[l], lnf, final)
        return h.reshape(b, ln_, D_MODEL), kd_all, vd_all, s_new, cnew[:, 8 - (CONV_W - 1):, :]

    hp, hs = x_prompt, x_sample
    pk_ = pv_ = sk_ = sv_ = None
    ps_, pc_, ss_, sc_ = [], [], [], []
    for l in range(depth):
        final = l == depth - 1
        hp, pk_, pv_, s_, c_ = layer(l, hp, cbuf_p, s0_p, None, None,
                                     mkb_all[l].reshape(bp, n_mem, D_MODEL),
                                     mvb_all[l].reshape(bp, n_mem, D_MODEL), pk_, pv_, final)
        ps_.append(s_); pc_.append(c_)
        hs, sk_, sv_, s_, c_ = layer(l, hs, cbuf_s_all[l], state_gdn[l], hist_k, hist_v,
                                     cmk_b[l], cmv_b[l], sk_, sv_, final)
        ss_.append(s_); sc_.append(c_)

    mem_shape = (depth, bp, n_mem, H_MEM, D_MEM)
    return (hp, hs,
            pk_.reshape(depth, bp, lp, H_DIFF, HD_DIFF), pv_.reshape(depth, bp, lp, H_DIFF, HD_DIFF),
            jnp.stack(ps_), jnp.stack(pc_),
            mk_all.reshape(mem_shape), mv_all.reshape(mem_shape),
            sk_.reshape(depth, bs, ls, H_DIFF, HD_DIFF), sv_.reshape(depth, bs, ls, H_DIFF, HD_DIFF),
            jnp.stack(ss_), jnp.stack(sc_))
```
